```python
import math
import jax, jax.numpy as jnp
from jax import lax
import numpy as np

D_MODEL = 2048
BATCH = 2
SEQ = 8192
DEPTH = 2

NORM_EPS = 1e-6
ATT_HEADS = 16
QK_NOPE = 128
QK_ROPE = 64
V_DIM = 128
KV_RANK = 512
ROPE_THETA = 10000.0
Q_BLOCK = 128
SSM_HEADS = 32
SSM_HEAD_DIM = 64
SSM_INNER = SSM_HEADS * SSM_HEAD_DIM
SSM_GROUPS = 4
SSM_STATE = 128
SSM_CONV = 5
SSM_CHUNK = 128
SCONV_DIM = D_MODEL
SCONV_K = 3
N_GROUPS = 8
EXPERTS_PER_GROUP = 4
N_EXPERTS = N_GROUPS * EXPERTS_PER_GROUP
TOP_K = 2
D_EXPERT = 512
N_BRANCH = 3
Q_DIM = ATT_HEADS * (QK_NOPE + QK_ROPE)
ATT_OUT = ATT_HEADS * V_DIM
XBC_DIM = SSM_INNER + 2 * SSM_GROUPS * SSM_STATE
SPLIT_SIZES = (Q_DIM, KV_RANK, QK_ROPE, SSM_INNER, XBC_DIM, 2 * SSM_HEADS,
               SCONV_DIM, SCONV_DIM, SCONV_DIM, N_BRANCH * D_MODEL)
IN_COLS = (Q_DIM + KV_RANK + QK_ROPE + SSM_INNER + XBC_DIM + 2 * SSM_HEADS
           + 3 * SCONV_DIM + N_BRANCH * D_MODEL)
BRANCH_IN = ATT_OUT + SSM_INNER + SCONV_DIM

kernel_name = "hybrid_mla_ssd_shortconv_hiermoe_encoder"


def rms_norm(x, w):
    xf = x.astype(jnp.float32)
    y = xf * lax.rsqrt(jnp.mean(xf * xf, axis=-1, keepdims=True) + NORM_EPS)
    return (y * w.astype(jnp.float32)).astype(x.dtype)


def split_cols(u):
    offsets = [int(o) for o in np.cumsum(SPLIT_SIZES)[:-1]]
    return jnp.split(u, offsets, axis=-1)


def depthwise_conv(x, w, pad):
    c = w.shape[-1]
    return lax.conv_general_dilated(
        x, w[:, None, :].astype(x.dtype), window_strides=(1,), padding=[(pad, pad)],
        dimension_numbers=('NWC', 'WIO', 'NWC'), feature_group_count=c)


def apply_rope(t, cos, sin):
    t1, t2 = jnp.split(t, 2, axis=-1)
    return jnp.concatenate([t1 * cos - t2 * sin, t1 * sin + t2 * cos], axis=-1)


def mla_attention(q, c_kv, k_rope, positions, kv_norm_w, w_uk, w_uv):
    b, s, _ = q.shape
    q = q.reshape(b, s, ATT_HEADS, QK_NOPE + QK_ROPE)
    q_nope, q_rope = q[..., :QK_NOPE], q[..., QK_NOPE:]
    c_kv = rms_norm(c_kv, kv_norm_w)
    k_nope = (c_kv @ w_uk).reshape(b, s, ATT_HEADS, QK_NOPE)
    v = (c_kv @ w_uv).reshape(b, s, ATT_HEADS, V_DIM)
    freqs = ROPE_THETA ** (-jnp.arange(0, QK_ROPE, 2, dtype=jnp.float32) / QK_ROPE)
    ang = positions.astype(jnp.float32)[..., None] * freqs
    cos, sin = jnp.cos(ang), jnp.sin(ang)
    q_rope = apply_rope(q_rope, cos[:, :, None, :], sin[:, :, None, :]).astype(q.dtype)
    k_rope = apply_rope(k_rope, cos, sin).astype(q.dtype)
    scale = (QK_NOPE + QK_ROPE) ** -0.5
    nblk = s // Q_BLOCK
    qn = jnp.moveaxis(q_nope.reshape(b, nblk, Q_BLOCK, ATT_HEADS, QK_NOPE), 1, 0)
    qr = jnp.moveaxis(q_rope.reshape(b, nblk, Q_BLOCK, ATT_HEADS, QK_ROPE), 1, 0)

    def block(args):
        qn_b, qr_b = args
        logits = (jnp.einsum('bqhd,bkhd->bhqk', qn_b, k_nope)
                  + jnp.einsum('bqhr,bkr->bhqk', qr_b, k_rope))
        p = jax.nn.softmax(logits.astype(jnp.float32) * scale, axis=-1)
        return jnp.einsum('bhqk,bkhv->bqhv', p.astype(v.dtype), v)

    out = lax.map(block, (qn, qr))
    return jnp.moveaxis(out, 0, 1).reshape(b, s, ATT_OUT)


def segsum(a):
    cs = jnp.cumsum(a, axis=-1)
    diff = cs[..., :, None] - cs[..., None, :]
    q = a.shape[-1]
    mask = jnp.tril(jnp.ones((q, q), dtype=bool))
    return jnp.where(mask, diff, -jnp.inf)


def ssd_scan(x, dt, A, B, C):
    b, s = x.shape[:2]
    nc = s // SSM_CHUNK
    r = SSM_HEADS // SSM_GROUPS
    xdt = (x * dt[..., None]).reshape(b, nc, SSM_CHUNK, SSM_GROUPS, r, SSM_HEAD_DIM)
    a = (dt * A).reshape(b, nc, SSM_CHUNK, SSM_GROUPS, r).transpose(0, 1, 3, 4, 2)
    Bc = B.reshape(b, nc, SSM_CHUNK, SSM_GROUPS, SSM_STATE)
    Cc = C.reshape(b, nc, SSM_CHUNK, SSM_GROUPS, SSM_STATE)
    a_cum = jnp.cumsum(a, axis=-1)
    decay_in = jnp.exp(segsum(a))
    cb = jnp.einsum('bclgn,bcsgn->bcgls', Cc, Bc)
    y_diag = jnp.einsum('bcgrls,bcsgrp->bclgrp', cb[:, :, :, None] * decay_in, xdt)
    decay_states = jnp.exp(a_cum[..., -1:] - a_cum)
    states = jnp.einsum('bclgn,bcgrl,bclgrp->bcgrpn', Bc, decay_states, xdt)
    chunk_decay = jnp.exp(a_cum[..., -1])

    def step(h, inp):
        st, dec = inp
        return dec[..., None, None] * h + st, h

    h0 = jnp.zeros_like(states[:, 0])
    _, prev = lax.scan(step, h0, (jnp.moveaxis(states, 1, 0), jnp.moveaxis(chunk_decay, 1, 0)))
    prev = jnp.moveaxis(prev, 0, 1)
    y_off = jnp.einsum('bclgn,bcgrpn,bcgrl->bclgrp', Cc, prev, jnp.exp(a_cum))
    return (y_diag + y_off).reshape(b, s, SSM_HEADS, SSM_HEAD_DIM)


def ssd_mixer(z, xbc, dt_raw, conv_w, conv_b, A_log, dt_bias, D_skip, norm_w):
    b, s, _ = z.shape
    f32 = jnp.float32
    xbc = jax.nn.silu(depthwise_conv(xbc, conv_w, SSM_CONV // 2) + conv_b)
    xs, Bm, Cm = jnp.split(xbc, [SSM_INNER, SSM_INNER + SSM_GROUPS * SSM_STATE], axis=-1)
    xs = xs.reshape(b, s, SSM_HEADS, SSM_HEAD_DIM).astype(f32)
    Bm = Bm.reshape(b, s, SSM_GROUPS, SSM_STATE).astype(f32)
    Cm = Cm.reshape(b, s, SSM_GROUPS, SSM_STATE).astype(f32)
    dt = jax.nn.softplus(dt_raw.astype(f32).reshape(b, s, 2, SSM_HEADS) + dt_bias.astype(f32))
    A = -jnp.exp(A_log.astype(f32))
    y_fwd = ssd_scan(xs, dt[:, :, 0], A[0], Bm, Cm)
    flip = lambda t: jnp.flip(t, axis=1)
    y_bwd = flip(ssd_scan(flip(xs), flip(dt[:, :, 1]), A[1], flip(Bm), flip(Cm)))
    y = y_fwd + y_bwd + D_skip.astype(f32)[:, None] * xs
    y = y.reshape(b, s, SSM_INNER) * jax.nn.silu(z.astype(f32))
    yg = y.reshape(b, s, SSM_GROUPS, SSM_INNER // SSM_GROUPS)
    yg = yg * lax.rsqrt(jnp.mean(yg * yg, axis=-1, keepdims=True) + NORM_EPS)
    return (yg.reshape(b, s, SSM_INNER) * norm_w.astype(f32)).astype(z.dtype)


def short_gated_conv(b_gate, c_gate, xs, w):
    return b_gate * depthwise_conv(c_gate * xs, w, SCONV_K // 2)


def hybrid_mixer(n, positions, w_in, kv_norm_w, w_uk, w_uv, ssm_conv_w, ssm_conv_b,
                 ssm_A_log, ssm_dt_bias, ssm_D, ssm_norm_w, sconv_w, w_branch, w_o):
    b, s, _ = n.shape
    u = n @ w_in
    q, c_kv, k_rope, z, xbc, dt_raw, sc_b, sc_c, sc_x, gate_logits = split_cols(u)
    y_att = mla_attention(q, c_kv, k_rope, positions, kv_norm_w, w_uk, w_uv)
    y_ssm = ssd_mixer(z, xbc, dt_raw, ssm_conv_w, ssm_conv_b, ssm_A_log, ssm_dt_bias, ssm_D, ssm_norm_w)
    y_conv = short_gated_conv(sc_b, sc_c, sc_x, sconv_w)
    p_att, p_ssm, p_conv = jnp.split(w_branch, [ATT_OUT, ATT_OUT + SSM_INNER], axis=0)
    g = jax.nn.sigmoid(gate_logits.astype(jnp.float32)).reshape(b, s, N_BRANCH, D_MODEL)
    merged = (g[:, :, 0] * (y_att @ p_att) + g[:, :, 1] * (y_ssm @ p_ssm)
              + g[:, :, 2] * (y_conv @ p_conv))
    return merged.astype(n.dtype) @ w_o


def hierarchical_moe(n, rg_w, rg_b, re_w, re_b, w_gate, w_up, w_down):
    b, s, d = n.shape
    f32 = jnp.float32
    t = n.reshape(b * s, d)
    g_logits = (t @ rg_w).astype(f32) + rg_b.astype(f32)
    g_prob = jax.nn.softmax(g_logits, axis=-1)
    g_idx = jnp.argmax(g_logits, axis=-1)
    g_w = jnp.take_along_axis(g_prob, g_idx[:, None], axis=-1)
    e_logits = ((t @ re_w).astype(f32) + re_b.astype(f32)).reshape(-1, N_GROUPS, EXPERTS_PER_GROUP)
    e_in_group = jnp.take_along_axis(e_logits, g_idx[:, None, None], axis=1)[:, 0]
    top_val, top_idx = lax.top_k(e_in_group, TOP_K)
    top_w = jax.nn.softmax(top_val, axis=-1) * g_w
    expert_id = g_idx[:, None] * EXPERTS_PER_GROUP + top_idx
    combine = jnp.sum(jax.nn.one_hot(expert_id, N_EXPERTS, dtype=f32) * top_w[..., None], axis=1)
    out = jnp.zeros((b * s, d), f32)
    for e in range(N_EXPERTS):
        ye = (jax.nn.silu(t @ w_gate[e]) * (t @ w_up[e])) @ w_down[e]
        out = out + combine[:, e:e + 1] * ye
    return out.reshape(b, s, d).astype(n.dtype)


def setup_inputs(seed: int = 0) -> dict:
    key = jax.random.key(seed)
    ks = jax.random.split(key, 25)
    f32 = jnp.float32
    L = DEPTH

    def nrm(k, shape, scale):
        return jax.random.normal(k, shape, f32) * scale

    def gain(k, shape):
        return 1.0 + 0.02 * jax.random.normal(k, shape, f32)

    x = nrm(ks[0], (BATCH, SEQ, D_MODEL), 1.0)
    positions = (jnp.arange(SEQ, dtype=jnp.int32)[None, :]
                 + jax.random.randint(ks[1], (BATCH, 1), 0, 1024, dtype=jnp.int32))
    dt0 = jnp.exp(jax.random.uniform(ks[10], (L, 2, SSM_HEADS), f32,
                                     minval=math.log(1e-3), maxval=math.log(1e-1)))
    return {
        "x": x,
        "positions": positions,
        "attn_norm_w": gain(ks[2], (L, D_MODEL)),
        "w_in": nrm(ks[3], (L, D_MODEL, IN_COLS), D_MODEL ** -0.5),
        "kv_norm_w": gain(ks[4], (L, KV_RANK)),
        "w_uk": nrm(ks[5], (L, KV_RANK, ATT_HEADS * QK_NOPE), KV_RANK ** -0.5),
        "w_uv": nrm(ks[6], (L, KV_RANK, ATT_HEADS * V_DIM), KV_RANK ** -0.5),
        "ssm_conv_w": nrm(ks[7], (L, SSM_CONV, XBC_DIM), SSM_CONV ** -0.5),
        "ssm_conv_b": nrm(ks[8], (L, XBC_DIM), 0.02),
        "ssm_A_log": jnp.log(jax.random.uniform(ks[9], (L, 2, SSM_HEADS), f32, minval=1.0, maxval=16.0)),
        "ssm_dt_bias": dt0 + jnp.log(-jnp.expm1(-dt0)),
        "ssm_D": gain(ks[11], (L, SSM_HEADS)),
        "ssm_norm_w": gain(ks[12], (L, SSM_INNER)),
        "sconv_w": nrm(ks[13], (L, SCONV_K, SCONV_DIM), SCONV_K ** -0.5),
        "w_branch": nrm(ks[14], (L, BRANCH_IN, D_MODEL), D_MODEL ** -0.5),
        "w_o": nrm(ks[15], (L, D_MODEL, D_MODEL), D_MODEL ** -0.5),
        "ffn_norm_w": gain(ks[16], (L, D_MODEL)),
        "router_group_w": nrm(ks[17], (L, D_MODEL, N_GROUPS), D_MODEL ** -0.5),
        "router_group_b": nrm(ks[18], (L, N_GROUPS), 0.01),
        "router_expert_w": nrm(ks[19], (L, D_MODEL, N_EXPERTS), D_MODEL ** -0.5),
        "router_expert_b": nrm(ks[20], (L, N_EXPERTS), 0.01),
        "expert_w_gate": nrm(ks[21], (L, N_EXPERTS, D_MODEL, D_EXPERT), D_MODEL ** -0.5),
        "expert_w_up": nrm(ks[22], (L, N_EXPERTS, D_MODEL, D_EXPERT), D_MODEL ** -0.5),
        "expert_w_down": nrm(ks[23], (L, N_EXPERTS, D_EXPERT, D_MODEL), D_EXPERT ** -0.5),
        "final_norm_w": gain(ks[24], (D_MODEL,)),
    }


def reference(x, positions, attn_norm_w, w_in, kv_norm_w, w_uk, w_uv, ssm_conv_w, ssm_conv_b,
              ssm_A_log, ssm_dt_bias, ssm_D, ssm_norm_w, sconv_w, w_branch, w_o, ffn_norm_w,
              router_group_w, router_group_b, router_expert_w, router_expert_b,
              expert_w_gate, expert_w_up, expert_w_down, final_norm_w):
    h = x
    for l in range(DEPTH):
        h = h + hybrid_mixer(rms_norm(h, attn_norm_w[l]), positions, w_in[l], kv_norm_w[l], w_uk[l],
                             w_uv[l], ssm_conv_w[l], ssm_conv_b[l], ssm_A_log[l], ssm_dt_bias[l],
                             ssm_D[l], ssm_norm_w[l], sconv_w[l], w_branch[l], w_o[l])
        h = h + hierarchical_moe(rms_norm(h, ffn_norm_w[l]), router_group_w[l], router_group_b[l],
                                 router_expert_w[l], router_expert_b[l], expert_w_gate[l],
                                 expert_w_up[l], expert_w_down[l])
    return rms_norm(h, final_norm_w)
```

```python
import functools

import numpy as np
import jax
import jax.numpy as jnp
from jax import lax
from jax.experimental import pallas as pl
from jax.experimental.pallas import tpu as pltpu

F32 = jnp.float32
BF16 = jnp.bfloat16
I32 = jnp.int32

D = 2048
EPS = 1e-6
HEADS = 16
NOPE = 128
ROPE = 64
VDIM = 128
KV_RANK = 512
ROPE_THETA = 10000.0
SSM_HEADS = 32
SSM_P = 64
SSM_INNER = SSM_HEADS * SSM_P
SSM_GROUPS = 4
SSM_N = 128
SSM_CONV = 5
CHUNK = 128
XBC = SSM_INNER + 2 * SSM_GROUPS * SSM_N
SCONV_K = 3
N_GROUPS = 8
EPG = 4
N_EXPERTS = N_GROUPS * EPG
D_EXPERT = 512
Q_DIM = HEADS * (NOPE + ROPE)
IN_COLS = Q_DIM + KV_RANK + ROPE + SSM_INNER + XBC + 2 * SSM_HEADS + 3 * D + 3 * D

LANES = 128
SUBLANES = 8
BF16_ROWS = 16
V7X_VMEM_BYTES = 64 * 1024 * 1024
VMEM_LIMIT = 56 * 1024 * 1024

OFF_QN = 0
OFF_QR = OFF_QN + HEADS * NOPE
OFF_CKV = OFF_QR + HEADS * ROPE
OFF_Z = OFF_CKV + KV_RANK
OFF_XBC = OFF_Z + SSM_INNER
OFF_SCB = OFF_XBC + XBC
OFF_SCC = OFF_SCB + D
OFF_SCX = OFF_SCC + D
OFF_GATE = OFF_SCX + D
OFF_SMALL = OFF_GATE + 3 * D
assert OFF_SMALL + LANES == IN_COLS

SLAB = D // LANES
PITCH = 24
MOE_TILE = 256


def _permute_in_proj(w):
    o_ckv = Q_DIM
    o_kr = o_ckv + KV_RANK
    o_z = o_kr + ROPE
    o_dt = o_z + SSM_INNER + XBC
    o_rest = o_dt + 2 * SSM_HEADS
    q = w[:, :Q_DIM].reshape(D, HEADS, NOPE + ROPE)
    parts = [q[:, :, :NOPE].reshape(D, HEADS * NOPE), q[:, :, NOPE:].reshape(D, HEADS * ROPE),
             w[:, o_ckv:o_kr], w[:, o_z:o_dt], w[:, o_rest:], w[:, o_kr:o_z], w[:, o_dt:o_rest]]
    out = jnp.concatenate([p.astype(BF16) for p in parts], axis=1)
    assert out.shape == (D, IN_COLS)
    return out


def _params(*sem):
    return pltpu.CompilerParams(dimension_semantics=tuple(sem), vmem_limit_bytes=VMEM_LIMIT)


def _dot(a, b):
    return jnp.dot(a, b, preferred_element_type=F32)


def _dot_nt(a, b):
    return lax.dot_general(a, b, (((1,), (1,)), ((), ())), preferred_element_type=F32)


def _dot_tn(a, b):
    return lax.dot_general(a, b, (((0,), (0,)), ((), ())), preferred_element_type=F32)


def _sigmoid(x):
    return 1.0 / (1.0 + jnp.exp(-x))


def _silu(x):
    return x * _sigmoid(x)


def _softplus(x):
    return jnp.maximum(x, 0.0) + jnp.log(1.0 + jnp.exp(-jnp.abs(x)))


def _lane_iota(shape):
    return lax.broadcasted_iota(I32, shape, len(shape) - 1)


def _rope_partner(v):
    lane = _lane_iota(v.shape)
    return jnp.where((lane & 32) == 0, pltpu.roll(v, LANES - 32, 1), pltpu.roll(v, 32, 1))


def _rope_tab_kernel(pos_ref, freq_ref, sign_ref, cs_ref, sn_ref):
    ang = pos_ref[...].astype(F32) * freq_ref[...]
    cs_ref[...] = jnp.cos(ang)
    sn_ref[...] = jnp.sin(ang) * sign_ref[...]


def _rope_tables(positions):
    t = positions.size
    tm = min(t, 1024)
    freqs = ROPE_THETA ** (-jnp.arange(0, ROPE, 2, dtype=F32) / ROPE)
    freq_row = jnp.tile(freqs, 4)[None, :]
    sign_row = jnp.tile(jnp.concatenate([-jnp.ones(32, F32), jnp.ones(32, F32)]), 2)[None, :]
    row = pl.BlockSpec((1, LANES), lambda i: (0, 0))
    tab = pl.BlockSpec((tm, LANES), lambda i: (i, 0))
    return pl.pallas_call(
        _rope_tab_kernel,
        out_shape=(jax.ShapeDtypeStruct((t, LANES), F32),) * 2,
        grid=(t // tm,),
        in_specs=[pl.BlockSpec((tm, 1), lambda i: (i, 0)), row, row],
        out_specs=(tab, tab),
        compiler_params=_params("parallel"),
        name="rope_tables",
    )(positions.reshape(t, 1), freq_row, sign_row)


def _rms(x, w):
    return x * lax.rsqrt(jnp.mean(x * x, axis=-1, keepdims=True) + EPS) * w


def _norm_kernel(h_ref, w_ref, n_ref):
    n_ref[...] = _rms(h_ref[...], w_ref[...]).astype(n_ref.dtype)


def _add_norm_kernel(h_ref, o0_ref, o1_ref, w_ref, hs_ref, n_ref):
    hs = h_ref[...] + o0_ref[...] + o1_ref[...]
    hs_ref[...] = hs
    n_ref[...] = _rms(hs, w_ref[...]).astype(n_ref.dtype)


def _norm(h, w, out_dtype):
    t = h.shape[0]
    tm = min(t, 256)
    blk = pl.BlockSpec((tm, D), lambda i: (i, 0))
    return pl.pallas_call(
        _norm_kernel,
        out_shape=jax.ShapeDtypeStruct((t, D), out_dtype),
        grid=(t // tm,),
        in_specs=[blk, pl.BlockSpec((1, D), lambda i: (0, 0))],
        out_specs=blk,
        compiler_params=_params("parallel"),
        name="rmsnorm",
    )(h, w.reshape(1, D))


def _add_norm(h, o, w, out_dtype):
    t = h.shape[0]
    tm = min(t, 256)
    blk = pl.BlockSpec((tm, D), lambda i: (i, 0))
    return pl.pallas_call(
        _add_norm_kernel,
        out_shape=(jax.ShapeDtypeStruct((t, D), F32), jax.ShapeDtypeStruct((t, D), out_dtype)),
        grid=(t // tm,),
        in_specs=[blk, pl.BlockSpec((None, tm, D), lambda i: (0, i, 0)),
                  pl.BlockSpec((None, tm, D), lambda i: (1, i, 0)),
                  pl.BlockSpec((1, D), lambda i: (0, 0))],
        out_specs=(blk, blk),
        compiler_params=_params("parallel"),
        name="add_rmsnorm",
    )(h, o, o, w.reshape(1, D))


def _inproj_kernel(n_ref, wm_ref, ws_ref, wst_ref, um_ref, us_ref, ust_ref):
    x = n_ref[...]
    um_ref[...] = _dot(x, wm_ref[...]).astype(um_ref.dtype)

    @pl.when(pl.program_id(1) == 0)
    def _():
        us_ref[...] = _dot(x, ws_ref[...])
        ust_ref[...] = _dot_nt(wst_ref[...], x)


def _in_proj(n, w_main, w_small, w_small_t):
    t = n.shape[0]
    tm = min(t, 1024)
    tn = 11 * LANES
    assert IN_COLS % tn == 0
    return pl.pallas_call(
        _inproj_kernel,
        out_shape=(jax.ShapeDtypeStruct((t, IN_COLS), BF16),
                   jax.ShapeDtypeStruct((t, LANES), F32),
                   jax.ShapeDtypeStruct((LANES, t), F32)),
        grid=(t // tm, IN_COLS // tn),
        in_specs=[pl.BlockSpec((tm, D), lambda i, j: (i, 0)),
                  pl.BlockSpec((D, tn), lambda i, j: (0, j)),
                  pl.BlockSpec((D, LANES), lambda i, j: (0, 0)),
                  pl.BlockSpec((LANES, D), lambda i, j: (0, 0))],
        out_specs=(pl.BlockSpec((tm, tn), lambda i, j: (i, j)),
                   pl.BlockSpec((tm, LANES), lambda i, j: (i, 0)),
                   pl.BlockSpec((LANES, tm), lambda i, j: (0, i))),
        compiler_params=_params("parallel", "arbitrary"),
        name="in_proj",
    )(n, w_main, w_small, w_small_t)


def _kv_kernel(ckv_ref, nw_ref, wuk_ref, wuv_ref, us_ref, cs_ref, sn_ref, k_ref, v_ref, c_sc):
    hh = pl.program_id(1)

    @pl.when(hh == 0)
    def _():
        c_sc[...] = _rms(ckv_ref[...].astype(F32), nw_ref[...]).astype(BF16)

    c = c_sc[...]
    k_ref[:, :NOPE] = _dot(c, wuk_ref[...]).astype(k_ref.dtype)
    v_ref[...] = _dot(c, wuv_ref[...]).astype(v_ref.dtype)
    u = us_ref[...]
    roped = u * cs_ref[...] + _rope_partner(u) * sn_ref[...]
    lane = _lane_iota(roped.shape)
    even = jnp.where(lane < ROPE, roped, 0.0)
    odd = jnp.where(lane >= ROPE, pltpu.roll(roped, ROPE, 1), 0.0)
    k_ref[:, NOPE:] = jnp.where(hh % 2 == 0, even, odd).astype(k_ref.dtype)


def _kv_proj(u_main, u_small, kv_norm_w, w_uk, w_uv, cs_tab, sn_tab, b, s):
    t = b * s
    tm = min(s, 1024)
    spt = s // tm
    tab = pl.BlockSpec((tm, LANES), lambda i, h: (i, 0))
    return pl.pallas_call(
        _kv_kernel,
        out_shape=(jax.ShapeDtypeStruct((b, HEADS, s, 2 * LANES), BF16),
                   jax.ShapeDtypeStruct((b, HEADS, s, VDIM), BF16)),
        grid=(t // tm, HEADS),
        in_specs=[pl.BlockSpec((tm, KV_RANK), lambda i, h: (i, OFF_CKV // KV_RANK)),
                  pl.BlockSpec((1, KV_RANK), lambda i, h: (0, 0)),
                  pl.BlockSpec((KV_RANK, NOPE), lambda i, h: (0, h)),
                  pl.BlockSpec((KV_RANK, VDIM), lambda i, h: (0, h)),
                  tab, tab, tab],
        out_specs=(pl.BlockSpec((None, None, tm, 2 * LANES), lambda i, h: (i // spt, h, i % spt, 0)),
                   pl.BlockSpec((None, None, tm, VDIM), lambda i, h: (i // spt, h, i % spt, 0))),
        scratch_shapes=[pltpu.VMEM((tm, KV_RANK), BF16)],
        compiler_params=_params("parallel", "arbitrary"),
        name="kv_proj",
    )(u_main, kv_norm_w.reshape(1, KV_RANK), w_uk, w_uv, u_small, cs_tab, sn_tab)


def _attn_kernel(qn_ref, qr_ref, cs_ref, sn_ref, k_ref, v_ref, o_ref, q_sc, m_sc, l_sc, acc_sc, *, tk, scale):
    qr = qr_ref[...].astype(F32)
    roped = qr * cs_ref[...] + _rope_partner(qr) * sn_ref[...]
    q_sc[:, :NOPE] = (qn_ref[...].astype(F32) * scale).astype(BF16)
    q_sc[:, NOPE:] = (roped * scale).astype(BF16)
    m_sc[...] = jnp.full(m_sc.shape, -jnp.inf, F32)
    l_sc[...] = jnp.zeros(l_sc.shape, F32)
    acc_sc[...] = jnp.zeros(acc_sc.shape, F32)

    def body(c, carry):
        start = pl.multiple_of(c * tk, tk)
        logits = _dot_nt(q_sc[...], k_ref[pl.ds(start, tk), :])
        m_prev = m_sc[...]
        m_new = jnp.maximum(m_prev, jnp.max(logits, axis=1, keepdims=True))
        alpha = jnp.exp(m_prev - m_new)
        p = jnp.exp(logits - m_new)
        l_sc[...] = alpha * l_sc[...] + jnp.sum(p, axis=1, keepdims=True)
        acc_sc[...] = alpha * acc_sc[...] + _dot(p.astype(BF16), v_ref[pl.ds(start, tk), :])
        m_sc[...] = m_new
        return carry

    lax.fori_loop(0, k_ref.shape[0] // tk, body, 0)
    o_ref[...] = (acc_sc[...] / l_sc[...]).astype(o_ref.dtype)


def _attention(u_main, k, v, cs_tab, sn_tab, b, s):
    t = b * s
    tq = min(s, 1024)
    tk = min(s, 512)
    qpt = s // tq
    tab = pl.BlockSpec((tq, LANES), lambda bb, h, qi: (bb * qpt + qi, 0))
    kern = functools.partial(_attn_kernel, tk=tk, scale=float(NOPE + ROPE) ** -0.5)
    return pl.pallas_call(
        kern,
        out_shape=jax.ShapeDtypeStruct((t, HEADS * VDIM), BF16),
        grid=(b, HEADS, qpt),
        in_specs=[pl.BlockSpec((tq, NOPE), lambda bb, h, qi: (bb * qpt + qi, h)),
                  pl.BlockSpec((tq, LANES), lambda bb, h, qi: (bb * qpt + qi, OFF_QR // LANES + h // 2)),
                  tab, tab,
                  pl.BlockSpec((None, None, s, 2 * LANES), lambda bb, h, qi: (bb, h, 0, 0)),
                  pl.BlockSpec((None, None, s, VDIM), lambda bb, h, qi: (bb, h, 0, 0))],
        out_specs=pl.BlockSpec((tq, VDIM), lambda bb, h, qi: (bb * qpt + qi, h)),
        scratch_shapes=[pltpu.VMEM((tq, 2 * LANES), BF16), pltpu.VMEM((tq, 1), F32),
                        pltpu.VMEM((tq, 1), F32), pltpu.VMEM((tq, VDIM), F32)],
        compiler_params=_params("parallel", "parallel", "arbitrary"),
        name="mla_attention",
    )(u_main, u_main, cs_tab, sn_tab, k, v)


HALO = BF16_ROWS
CONV_COLS = 512
assert all(o % CONV_COLS == 0 for o in (OFF_XBC, OFF_SCB, OFF_SCC, OFF_SCX))


def _fill_ext(ext_ref, prev, cur, nxt, first, last):
    tm = cur.shape[0]
    ext_ref[:HALO, :] = jnp.where(first, 0.0, prev.astype(F32))
    ext_ref[HALO:HALO + tm, :] = cur.astype(F32)
    ext_ref[HALO + tm:, :] = jnp.where(last, 0.0, nxt.astype(F32))


def _taps(ext_ref, w_ref, ksize, tm):
    pad = ksize // 2
    acc = None
    for kk in range(ksize):
        term = ext_ref[pl.ds(HALO - pad + kk, tm), :] * w_ref[kk:kk + 1, :]
        acc = term if acc is None else acc + term
    return acc


def _ssm_conv_kernel(prev_ref, cur_ref, next_ref, w_ref, b_ref, o_ref, ext_ref, *, tiles_per_seq):
    i = pl.program_id(0)
    tm = cur_ref.shape[0]
    _fill_ext(ext_ref, prev_ref[...], cur_ref[...], next_ref[...],
              i % tiles_per_seq == 0, i % tiles_per_seq == tiles_per_seq - 1)
    o_ref[...] = _silu(_taps(ext_ref, w_ref, SSM_CONV, tm) + b_ref[...]).astype(o_ref.dtype)


def _halo_specs(tm, tc, col0, n_rows):
    r = tm // HALO
    last = n_rows // HALO - 1
    cb = col0 // tc
    return [pl.BlockSpec((HALO, tc), lambda i, j: (jnp.maximum(i * r - 1, 0), cb + j)),
            pl.BlockSpec((tm, tc), lambda i, j: (i, cb + j)),
            pl.BlockSpec((HALO, tc), lambda i, j: (jnp.minimum((i + 1) * r, last), cb + j))]


def _ssm_conv(u_main, conv_w, conv_b, s):
    t = u_main.shape[0]
    tm = min(s, 512)
    tc = CONV_COLS
    kern = functools.partial(_ssm_conv_kernel, tiles_per_seq=s // tm)
    return pl.pallas_call(
        kern,
        out_shape=jax.ShapeDtypeStruct((t, XBC), BF16),
        grid=(t // tm, XBC // tc),
        in_specs=_halo_specs(tm, tc, OFF_XBC, t) + [
            pl.BlockSpec((SSM_CONV, tc), lambda i, j: (0, j)),
            pl.BlockSpec((1, tc), lambda i, j: (0, j))],
        out_specs=pl.BlockSpec((tm, tc), lambda i, j: (i, j)),
        scratch_shapes=[pltpu.VMEM((tm + 2 * HALO, tc), F32)],
        compiler_params=_params("parallel", "parallel"),
        name="ssm_conv",
    )(u_main, u_main, u_main, conv_w, conv_b.reshape(1, XBC))


def _sconv_kernel(cp_ref, cc_ref, cn_ref, xp_ref, xc_ref, xn_ref, b_ref, w_ref, o_ref, ext_ref, *, tiles_per_seq):
    i = pl.program_id(0)
    tm = cc_ref.shape[0]
    f = lambda r: r[...].astype(F32)
    _fill_ext(ext_ref, f(cp_ref) * f(xp_ref), f(cc_ref) * f(xc_ref), f(cn_ref) * f(xn_ref),
              i % tiles_per_seq == 0, i % tiles_per_seq == tiles_per_seq - 1)
    o_ref[...] = (f(b_ref) * _taps(ext_ref, w_ref, SCONV_K, tm)).astype(o_ref.dtype)


def _short_conv(u_main, sconv_w, s):
    t = u_main.shape[0]
    tm = min(s, 512)
    tc = CONV_COLS
    kern = functools.partial(_sconv_kernel, tiles_per_seq=s // tm)
    return pl.pallas_call(
        kern,
        out_shape=jax.ShapeDtypeStruct((t, D), BF16),
        grid=(t // tm, D // tc),
        in_specs=_halo_specs(tm, tc, OFF_SCC, t) + _halo_specs(tm, tc, OFF_SCX, t) + [
            pl.BlockSpec((tm, tc), lambda i, j: (i, OFF_SCB // tc + j)),
            pl.BlockSpec((SCONV_K, tc), lambda i, j: (0, j))],
        out_specs=pl.BlockSpec((tm, tc), lambda i, j: (i, j)),
        scratch_shapes=[pltpu.VMEM((tm + 2 * HALO, tc), F32)],
        compiler_params=_params("parallel", "parallel"),
        name="short_conv",
    )(u_main, u_main, u_main, u_main, u_main, u_main, u_main, sconv_w)


FWD_COL = 2 * SSM_HEADS
BWD_COL = 3 * SSM_HEADS
PAIRS = SSM_HEADS // 2
PAIRS_PER_GROUP = PAIRS // SSM_GROUPS


def _cumsum(x, axis, reverse):
    n = x.shape[axis]
    idx = lax.broadcasted_iota(I32, x.shape, axis)
    k = 1
    while k < n:
        if reverse:
            x = x + jnp.where(idx < n - k, pltpu.roll(x, n - k, axis), 0.0)
        else:
            x = x + jnp.where(idx >= k, pltpu.roll(x, k, axis), 0.0)
        k *= 2
    return x


def _ssd_direction(xbc_ref, us_ref, ust_ref, dtb_row, a_row, dtb_col, a_col, d_row, y_ref, st_ref, *, reverse):
    base = BWD_COL if reverse else FWD_COL
    edge = 0 if reverse else CHUNK - 1
    dt_c = _softplus(us_ref[...] + dtb_row)
    cum_c = _cumsum(dt_c * a_row, 0, reverse)
    dt_r = _softplus(ust_ref[...] + dtb_col)
    cum_r = _cumsum(dt_r * a_col, 1, reverse)
    ecum_c = jnp.exp(cum_c)
    tot = cum_c[edge:edge + 1, :]
    dec_c = jnp.exp(tot - cum_c) * dt_c
    etot = jnp.exp(tot)

    li = lax.broadcasted_iota(I32, (CHUNK, CHUNK), 0)
    si = lax.broadcasted_iota(I32, (CHUNK, CHUNK), 1)
    mask = (li <= si) if reverse else (li >= si)
    lane = _lane_iota((CHUNK, LANES))
    lo = lane < SSM_P

    for g in range(SSM_GROUPS):
        bg = xbc_ref[:, SSM_INNER + g * SSM_N:SSM_INNER + (g + 1) * SSM_N]
        cg = xbc_ref[:, SSM_INNER + (SSM_GROUPS + g) * SSM_N:SSM_INNER + (SSM_GROUPS + g + 1) * SSM_N]
        cb = _dot_nt(cg, bg)
        for j in range(PAIRS_PER_GROUP):
            pr = g * PAIRS_PER_GROUP + j
            c1, c2 = base + 2 * pr, base + 2 * pr + 1
            cols = slice(pr * LANES, (pr + 1) * LANES)

            def pair(arr):
                return jnp.where(lo, arr[:, c1:c1 + 1], arr[:, c2:c2 + 1])

            xs = xbc_ref[:, cols].astype(F32)
            xdt = xs * pair(dt_c)
            scores = []
            for c in (c1, c2):
                seg = cum_c[:, c:c + 1] - cum_r[c:c + 1, :]
                scores.append((cb * jnp.exp(jnp.where(mask, seg, -jnp.inf))).astype(BF16))
            xb = xdt.astype(BF16)
            zero = jnp.zeros_like(xb)
            x_bd = jnp.concatenate([jnp.where(lo, xb, zero), jnp.where(lo, zero, xb)], axis=0)
            y = _dot(jnp.concatenate(scores, axis=1), x_bd)
            st = st_ref[g, :, j * LANES:(j + 1) * LANES]
            y = y + _dot(cg, st.astype(BF16)) * pair(ecum_c)
            if not reverse:
                y = y + xs * d_row[:, cols]
            y_ref[:, cols] = y.astype(y_ref.dtype)
            new = _dot_tn(bg, (xs * pair(dec_c)).astype(BF16))
            st_ref[g, :, j * LANES:(j + 1) * LANES] = st * pair(etot) + new


def _ssd_kernel(xf_ref, usf_ref, ustf_ref, xb_ref, usb_ref, ustb_ref, dtb_row_ref, alog_row_ref,
                dtb_col_ref, alog_col_ref, d_ref, yf_ref, yb_ref, stf_ref, stb_ref):
    @pl.when(pl.program_id(1) == 0)
    def _():
        stf_ref[...] = jnp.zeros(stf_ref.shape, F32)
        stb_ref[...] = jnp.zeros(stb_ref.shape, F32)

    dtb_row, dtb_col = dtb_row_ref[...], dtb_col_ref[...]
    a_row, a_col = -jnp.exp(alog_row_ref[...]), -jnp.exp(alog_col_ref[...])
    d_row = d_ref[...]
    _ssd_direction(xf_ref, usf_ref, ustf_ref, dtb_row, a_row, dtb_col, a_col, d_row, yf_ref, stf_ref, reverse=False)
    _ssd_direction(xb_ref, usb_ref, ustb_ref, dtb_row, a_row, dtb_col, a_col, d_row, yb_ref, stb_ref, reverse=True)


def _ssd_scan(xbc_conv, u_small, u_small_t, a_log, dt_bias, d_skip, b, s):
    t = b * s
    nc = s // CHUNK
    pad = jnp.zeros((2 * SSM_HEADS,), F32)
    dtb = jnp.concatenate([pad, dt_bias.reshape(-1)])
    alog = jnp.concatenate([pad, a_log.reshape(-1)])
    d_row = jnp.repeat(d_skip, SSM_P)[None, :]
    fw = lambda bb, i: bb * nc + i
    bw = lambda bb, i: bb * nc + nc - 1 - i
    row = lambda w: pl.BlockSpec((1, w), lambda bb, i: (0, 0))
    col = pl.BlockSpec((LANES, 1), lambda bb, i: (0, 0))

    def chunk_specs(idx):
        return [pl.BlockSpec((CHUNK, XBC), lambda bb, i: (idx(bb, i), 0)),
                pl.BlockSpec((CHUNK, LANES), lambda bb, i: (idx(bb, i), 0)),
                pl.BlockSpec((LANES, CHUNK), lambda bb, i: (0, idx(bb, i)))]

    return pl.pallas_call(
        _ssd_kernel,
        out_shape=(jax.ShapeDtypeStruct((t, SSM_INNER), BF16),) * 2,
        grid=(b, nc),
        in_specs=chunk_specs(fw) + chunk_specs(bw) + [row(LANES), row(LANES), col, col, row(SSM_INNER)],
        out_specs=(pl.BlockSpec((CHUNK, SSM_INNER), lambda bb, i: (fw(bb, i), 0)),
                   pl.BlockSpec((CHUNK, SSM_INNER), lambda bb, i: (bw(bb, i), 0))),
        scratch_shapes=[pltpu.VMEM((SSM_GROUPS, SSM_N, SSM_INNER // SSM_GROUPS), F32)] * 2,
        compiler_params=_params("parallel", "arbitrary"),
        name="ssd_scan",
    )(xbc_conv, u_small, u_small_t, xbc_conv, u_small, u_small_t,
      dtb[None, :], alog[None, :], dtb[:, None], alog[:, None], d_row)


def _ssm_post_kernel(yf_ref, yb_ref, z_ref, w_ref, o_ref):
    y = (yf_ref[...].astype(F32) + yb_ref[...].astype(F32)) * _silu(z_ref[...].astype(F32))
    o_ref[...] = _rms(y, w_ref[...]).astype(o_ref.dtype)


def _ssm_post(y_fwd, y_bwd, u_main, norm_w):
    t = y_fwd.shape[0]
    tm = min(t, 512)
    gw = SSM_INNER // SSM_GROUPS
    blk = pl.BlockSpec((tm, gw), lambda i, g: (i, g))
    return pl.pallas_call(
        _ssm_post_kernel,
        out_shape=jax.ShapeDtypeStruct((t, SSM_INNER), BF16),
        grid=(t // tm, SSM_GROUPS),
        in_specs=[blk, blk, pl.BlockSpec((tm, gw), lambda i, g: (i, OFF_Z // gw + g)),
                  pl.BlockSpec((1, gw), lambda i, g: (0, g))],
        out_specs=blk,
        compiler_params=_params("parallel", "parallel"),
        name="ssm_gate_norm",
    )(y_fwd, y_bwd, u_main, norm_w.reshape(1, SSM_INNER))


def _merge_kernel(ya_ref, ys_ref, yc_ref, pa_ref, ps_ref, pc_ref, ga_ref, gs_ref, gc_ref, o_ref):
    g = lambda r: _sigmoid(r[...].astype(F32))
    m = (g(ga_ref) * _dot(ya_ref[...], pa_ref[...]) + g(gs_ref) * _dot(ys_ref[...], ps_ref[...])
         + g(gc_ref) * _dot(yc_ref[...], pc_ref[...]))
    o_ref[...] = m.astype(o_ref.dtype)


def _merge(y_att, y_ssm, y_conv, w_branch, u_main):
    t = y_att.shape[0]
    tm = min(t, 512)
    tn = 512
    yb = pl.BlockSpec((tm, D), lambda i, j: (i, 0))
    pspec = lambda r: pl.BlockSpec((D, tn), lambda i, j: (r, j))
    gspec = lambda r: pl.BlockSpec((tm, tn), lambda i, j: (i, OFF_GATE // tn + r * (D // tn) + j))
    return pl.pallas_call(
        _merge_kernel,
        out_shape=jax.ShapeDtypeStruct((t, D), BF16),
        grid=(t // tm, D // tn),
        in_specs=[yb, yb, yb, pspec(0), pspec(1), pspec(2), gspec(0), gspec(1), gspec(2)],
        out_specs=pl.BlockSpec((tm, tn), lambda i, j: (i, j)),
        compiler_params=_params("parallel", "arbitrary"),
        name="branch_merge",
    )(y_att, y_ssm, y_conv, w_branch, w_branch, w_branch, u_main, u_main, u_main)


def _split_bf16(x):
    hi = x.astype(BF16)
    lo = (x - hi.astype(F32)).astype(BF16)
    return hi, lo


def _wo_kernel(m_ref, h_ref, wo_ref, nw_ref, rhi_ref, rlo_ref, rb_ref, h1_ref, t_ref, lg_ref):
    h1 = h_ref[...] + _dot(m_ref[...], wo_ref[...])
    h1_ref[...] = h1
    tn = _rms(h1, nw_ref[...])
    t_ref[...] = tn
    hi, lo = _split_bf16(tn)
    lg_ref[...] = _dot(hi, rhi_ref[...]) + _dot(lo, rhi_ref[...]) + _dot(hi, rlo_ref[...]) + rb_ref[...]


def _out_proj(merged, h, w_o, ffn_norm_w, r_hi, r_lo, r_bias):
    t = h.shape[0]
    tm = min(t, 256)
    blk = pl.BlockSpec((tm, D), lambda i: (i, 0))
    const = lambda shape: pl.BlockSpec(shape, lambda i: (0, 0))
    lg = pl.BlockSpec((tm, LANES), lambda i: (i, 0))
    return pl.pallas_call(
        _wo_kernel,
        out_shape=(jax.ShapeDtypeStruct((t, D), F32), jax.ShapeDtypeStruct((t, D), F32),
                   jax.ShapeDtypeStruct((t, LANES), F32)),
        grid=(t // tm,),
        in_specs=[blk, blk, const((D, D)), const((1, D)), const((D, LANES)), const((D, LANES)),
                  const((1, LANES))],
        out_specs=(blk, blk, lg),
        compiler_params=_params("parallel"),
        name="out_proj_router",
    )(merged, h, w_o, ffn_norm_w.reshape(1, D), r_hi, r_lo, r_bias)


def _route_kernel(lg_ref, id_ref, wt_ref):
    lg = lg_ref[...]
    lane = _lane_iota(lg.shape).astype(F32)
    big = float(LANES)
    ninf = -jnp.inf
    first = lambda hit: jnp.min(jnp.where(hit, lane, big), axis=1, keepdims=True)

    gl = jnp.where(lane < N_GROUPS, lg, ninf)
    gmax = jnp.max(gl, axis=1, keepdims=True)
    gidx = first(gl == gmax)
    g_w = 1.0 / jnp.sum(jnp.exp(gl - gmax), axis=1, keepdims=True)

    in_group = (lane >= N_GROUPS + gidx * EPG) & (lane < N_GROUPS + (gidx + 1.0) * EPG)
    el = jnp.where(in_group, lg, ninf)
    v0 = jnp.max(el, axis=1, keepdims=True)
    i0 = first(el == v0)
    el = jnp.where(lane == i0, ninf, el)
    v1 = jnp.max(el, axis=1, keepdims=True)
    i1 = first(el == v1)
    e1 = jnp.exp(v1 - v0)
    w0 = g_w / (1.0 + e1)
    w1 = g_w * e1 / (1.0 + e1)
    id_ref[...] = jnp.where(lane == 0.0, i0, i1).astype(I32) - N_GROUPS
    wt_ref[...] = jnp.where(lane == 0.0, w0, w1)


def _route(logits):
    t = logits.shape[0]
    tm = min(t, 1024)
    blk = pl.BlockSpec((tm, LANES), lambda i: (i, 0))
    return pl.pallas_call(
        _route_kernel,
        out_shape=(jax.ShapeDtypeStruct((t, LANES), I32), jax.ShapeDtypeStruct((t, LANES), F32)),
        grid=(t // tm,),
        in_specs=[blk],
        out_specs=(blk, blk),
        compiler_params=_params("parallel"),
        name="route",
    )(logits)


def _dispatch_plan(ids, wts, t, tile):
    n_rows = 2 * t
    n_tiles = n_rows // tile + N_EXPERTS
    e = ids[:, :2].reshape(-1)
    w = wts[:, :2].reshape(-1)
    onehot = (e[:, None] == jnp.arange(N_EXPERTS, dtype=I32)[None, :]).astype(I32)
    csum = jnp.cumsum(onehot, axis=0)
    rank = jnp.take_along_axis(csum, e[:, None], axis=1)[:, 0] - 1
    counts = csum[-1]
    padded = ((counts + tile - 1) // tile) * tile
    ends = jnp.cumsum(padded)
    pos = (ends - padded)[e] + rank
    r = jnp.arange(n_rows, dtype=I32)
    row_tok = jnp.zeros((n_tiles * tile,), I32).at[pos].set(r // 2)
    row_dst = jnp.full((n_tiles * tile,), -1, I32).at[pos].set((r % 2) * t + r // 2)
    row_w = jnp.zeros((n_tiles * tile,), F32).at[pos].set(w)
    tile_start = jnp.arange(n_tiles, dtype=I32) * tile
    tile_e = jnp.minimum(jnp.searchsorted(ends, tile_start, side="right"), N_EXPERTS - 1).astype(I32)
    n_used = (ends[-1] // tile).astype(I32).reshape(1)
    return (row_tok.reshape(n_tiles, 1, tile), row_dst.reshape(n_tiles, 1, tile),
            row_w.reshape(n_tiles * tile, 1), tile_e, n_used)


def _moe_kernel(te_ref, nu_ref, tok_ref, dst_ref, rw_ref, t_hbm, wg_ref, wu_ref, wd_ref, o_hbm,
                xbuf, ybuf, wg_sc, wu_sc, wd_sc, gsem, ssem):
    i = pl.program_id(0)
    tile = rw_ref.shape[0]

    def slab(buf, r):
        return buf.at[pl.ds(pl.multiple_of(r * PITCH, SUBLANES), SLAB), :]

    def gather(r):
        return pltpu.make_async_copy(t_hbm.at[tok_ref[0, 0, r]], slab(xbuf, r), gsem)

    def scatter(r):
        return pltpu.make_async_copy(slab(ybuf, r), o_hbm.at[jnp.maximum(dst_ref[0, 0, r], 0)], ssem)

    @pl.when(i < nu_ref[0])
    def _():
        def g_start(r, c):
            gather(r).start()
            return c

        lax.fori_loop(0, tile, g_start, 0)

        @pl.when((i == 0) | (te_ref[i] != te_ref[jnp.maximum(i - 1, 0)]))
        def _():
            wg_sc[...] = wg_ref[...].astype(BF16)
            wu_sc[...] = wu_ref[...].astype(BF16)
            wd_sc[...] = wd_ref[...].astype(BF16)

        def g_wait(r, c):
            gather(r).wait()
            return c

        lax.fori_loop(0, tile, g_wait, 0)

        x = jnp.concatenate([xbuf[pl.ds(sl, tile, stride=PITCH), :] for sl in range(SLAB)], axis=1).astype(BF16)
        gate = _dot(x, wg_sc[...])
        mid = (_silu(gate) * _dot(x, wu_sc[...])).astype(BF16)
        y = _dot(mid, wd_sc[...]) * rw_ref[...]
        for sl in range(SLAB):
            ybuf[pl.ds(sl, tile, stride=PITCH), :] = y[:, sl * LANES:(sl + 1) * LANES]

        def s_start(r, c):
            @pl.when(dst_ref[0, 0, r] >= 0)
            def _():
                scatter(r).start()
            return c

        lax.fori_loop(0, tile, s_start, 0)

        def s_wait(r, c):
            @pl.when(dst_ref[0, 0, r] >= 0)
            def _():
                scatter(r).wait()
            return c

        lax.fori_loop(0, tile, s_wait, 0)


def _moe(t_norm, plan, w_gate, w_up, w_down):
    t = t_norm.shape[0]
    row_tok, row_dst, row_w, tile_e, n_used = plan
    n_tiles, _, tile = row_tok.shape
    t_slab = t_norm.reshape(t, SLAB, LANES)
    smem = pl.BlockSpec((1, 1, tile), lambda i, te, nu: (i, 0, 0), memory_space=pltpu.SMEM)
    wspec = lambda a, b: pl.BlockSpec((None, a, b), lambda i, te, nu: (te[i], 0, 0))
    out = pl.pallas_call(
        _moe_kernel,
        out_shape=jax.ShapeDtypeStruct((2 * t, SLAB, LANES), F32),
        grid_spec=pltpu.PrefetchScalarGridSpec(
            num_scalar_prefetch=2,
            grid=(n_tiles,),
            in_specs=[smem, smem,
                      pl.BlockSpec((tile, 1), lambda i, te, nu: (i, 0)),
                      pl.BlockSpec(memory_space=pl.ANY),
                      wspec(D, D_EXPERT), wspec(D, D_EXPERT), wspec(D_EXPERT, D)],
            out_specs=pl.BlockSpec(memory_space=pl.ANY),
            scratch_shapes=[pltpu.VMEM((tile * PITCH, LANES), F32), pltpu.VMEM((tile * PITCH, LANES), F32),
                            pltpu.VMEM((D, D_EXPERT), BF16), pltpu.VMEM((D, D_EXPERT), BF16),
                            pltpu.VMEM((D_EXPERT, D), BF16),
                            pltpu.SemaphoreType.DMA(()), pltpu.SemaphoreType.DMA(())]),
        compiler_params=_params("arbitrary"),
        name="moe_experts",
    )(tile_e, n_used, row_tok, row_dst, row_w, t_slab, w_gate, w_up, w_down)
    return out.reshape(2, t, D)


def kernel(x, positions, attn_norm_w, w_in, kv_norm_w, w_uk, w_uv, ssm_conv_w, ssm_conv_b, ssm_A_log, ssm_dt_bias, ssm_D, ssm_norm_w, sconv_w, w_branch, w_o, ffn_norm_w, router_group_w, router_group_b, router_expert_w, router_expert_b, expert_w_gate, expert_w_up, expert_w_down, final_norm_w):
    b, s, _ = x.shape
    t = b * s
    depth = w_in.shape[0]
    h = x.reshape(t, D)
    cs_tab, sn_tab = _rope_tables(positions)
    moe_out = None
    for l in range(depth):
        if moe_out is None:
            n = _norm(h, attn_norm_w[l], BF16)
        else:
            h, n = _add_norm(h, moe_out, attn_norm_w[l], BF16)
        w_perm = _permute_in_proj(w_in[l])
        w_small = w_perm[:, OFF_SMALL:]
        u_main, u_small, u_small_t = _in_proj(n, w_perm, w_small, w_small.T)

        k, v = _kv_proj(u_main, u_small, kv_norm_w[l], w_uk[l].astype(BF16), w_uv[l].astype(BF16),
                        cs_tab, sn_tab, b, s)
        y_att = _attention(u_main, k, v, cs_tab, sn_tab, b, s)

        xbc_conv = _ssm_conv(u_main, ssm_conv_w[l], ssm_conv_b[l], s)
        y_fwd, y_bwd = _ssd_scan(xbc_conv, u_small, u_small_t, ssm_A_log[l], ssm_dt_bias[l], ssm_D[l], b, s)
        y_ssm = _ssm_post(y_fwd, y_bwd, u_main, ssm_norm_w[l])

        y_conv = _short_conv(u_main, sconv_w[l], s)

        merged = _merge(y_att, y_ssm, y_conv, w_branch[l].astype(BF16), u_main)

        r_w = jnp.concatenate([router_group_w[l], router_expert_w[l],
                               jnp.zeros((D, LANES - N_GROUPS - N_EXPERTS), F32)], axis=1)
        r_b = jnp.concatenate([router_group_b[l], router_expert_b[l],
                               jnp.zeros((LANES - N_GROUPS - N_EXPERTS,), F32)])[None, :]
        r_hi, r_lo = _split_bf16(r_w)
        h, t_norm, logits = _out_proj(merged, h, w_o[l].astype(BF16), ffn_norm_w[l], r_hi, r_lo, r_b)

        ids, wts = _route(logits)
        plan = _dispatch_plan(ids, wts, t, MOE_TILE)
        moe_out = _moe(t_norm, plan, expert_w_gate[l], expert_w_up[l], expert_w_down[l])

    h, out = _add_norm(h, moe_out, final_norm_w, F32)
    return out.reshape(b, s, D)
```

```python
import functools

import numpy as np
import jax
import jax.numpy as jnp
from jax import lax
from jax.experimental import pallas as pl
from jax.experimental.pallas import tpu as pltpu

F32 = jnp.float32
BF16 = jnp.bfloat16
I32 = jnp.int32

D = 2048
EPS = 1e-6
HEADS = 16
NOPE = 128
ROPE = 64
VDIM = 128
KV_RANK = 512
ROPE_THETA = 10000.0
SSM_HEADS = 32
SSM_P = 64
SSM_INNER = SSM_HEADS * SSM_P
SSM_GROUPS = 4
SSM_N = 128
SSM_CONV = 5
CHUNK = 128
XBC = SSM_INNER + 2 * SSM_GROUPS * SSM_N
SCONV_K = 3
N_GROUPS = 8
EPG = 4
N_EXPERTS = N_GROUPS * EPG
D_EXPERT = 512
Q_DIM = HEADS * (NOPE + ROPE)
IN_COLS = Q_DIM + KV_RANK + ROPE + SSM_INNER + XBC + 2 * SSM_HEADS + 3 * D + 3 * D

LANES = 128
SUBLANES = 8
BF16_ROWS = 16
V7X_VMEM_BYTES = 64 * 1024 * 1024
VMEM_LIMIT = 56 * 1024 * 1024

OFF_QN = 0
OFF_QR = OFF_QN + HEADS * NOPE
OFF_CKV = OFF_QR + HEADS * ROPE
OFF_Z = OFF_CKV + KV_RANK
OFF_XBC = OFF_Z + SSM_INNER
OFF_SCB = OFF_XBC + XBC
OFF_SCC = OFF_SCB + D
OFF_SCX = OFF_SCC + D
OFF_GATE = OFF_SCX + D
OFF_SMALL = OFF_GATE + 3 * D
assert OFF_SMALL + LANES == IN_COLS

SLAB = D // LANES
PITCH = 24
MOE_TILE = 256


def _permute_in_proj(w):
    o_ckv = Q_DIM
    o_kr = o_ckv + KV_RANK
    o_z = o_kr + ROPE
    o_dt = o_z + SSM_INNER + XBC
    o_rest = o_dt + 2 * SSM_HEADS
    q = w[:, :Q_DIM].reshape(D, HEADS, NOPE + ROPE)
    parts = [q[:, :, :NOPE].reshape(D, HEADS * NOPE), q[:, :, NOPE:].reshape(D, HEADS * ROPE),
             w[:, o_ckv:o_kr], w[:, o_z:o_dt], w[:, o_rest:], w[:, o_kr:o_z], w[:, o_dt:o_rest]]
    out = jnp.concatenate([p.astype(BF16) for p in parts], axis=1)
    assert out.shape == (D, IN_COLS)
    return out


def _params(*sem):
    return pltpu.CompilerParams(dimension_semantics=tuple(sem), vmem_limit_bytes=VMEM_LIMIT)


def _dot(a, b):
    return jnp.dot(a, b, preferred_element_type=F32)


def _dot_nt(a, b):
    return lax.dot_general(a, b, (((1,), (1,)), ((), ())), preferred_element_type=F32)


def _dot_tn(a, b):
    return lax.dot_general(a, b, (((0,), (0,)), ((), ())), preferred_element_type=F32)


def _sigmoid(x):
    return 1.0 / (1.0 + jnp.exp(-x))


def _silu(x):
    return x * _sigmoid(x)


def _softplus(x):
    return jnp.maximum(x, 0.0) + jnp.log(1.0 + jnp.exp(-jnp.abs(x)))


def _lane_iota(shape):
    return lax.broadcasted_iota(I32, shape, len(shape) - 1)


def _rope_partner(v):
    lane = _lane_iota(v.shape)
    return jnp.where((lane & 32) == 0, pltpu.roll(v, LANES - 32, 1), pltpu.roll(v, 32, 1))


def _rope_tab_kernel(pos_ref, freq_ref, sign_ref, cs_ref, sn_ref):
    ang = pos_ref[...].astype(F32) * freq_ref[...]
    cs_ref[...] = jnp.cos(ang)
    sn_ref[...] = jnp.sin(ang) * sign_ref[...]


def _rope_tables(positions):
    t = positions.size
    tm = min(t, 1024)
    freqs = ROPE_THETA ** (-jnp.arange(0, ROPE, 2, dtype=F32) / ROPE)
    freq_row = jnp.tile(freqs, 4)[None, :]
    sign_row = jnp.tile(jnp.concatenate([-jnp.ones(32, F32), jnp.ones(32, F32)]), 2)[None, :]
    row = pl.BlockSpec((1, LANES), lambda i: (0, 0))
    tab = pl.BlockSpec((tm, LANES), lambda i: (i, 0))
    return pl.pallas_call(
        _rope_tab_kernel,
        out_shape=(jax.ShapeDtypeStruct((t, LANES), F32),) * 2,
        grid=(t // tm,),
        in_specs=[pl.BlockSpec((tm, 1), lambda i: (i, 0)), row, row],
        out_specs=(tab, tab),
        compiler_params=_params("parallel"),
        name="rope_tables",
    )(positions.reshape(t, 1), freq_row, sign_row)


def _rms(x, w):
    return x * lax.rsqrt(jnp.mean(x * x, axis=-1, keepdims=True) + EPS) * w


def _norm_kernel(h_ref, w_ref, n_ref):
    n_ref[...] = _rms(h_ref[...], w_ref[...]).astype(n_ref.dtype)


def _add_norm_kernel(h_ref, o0_ref, o1_ref, w_ref, hs_ref, n_ref):
    hs = h_ref[...] + o0_ref[...] + o1_ref[...]
    hs_ref[...] = hs
    n_ref[...] = _rms(hs, w_ref[...]).astype(n_ref.dtype)


def _norm(h, w, out_dtype):
    t = h.shape[0]
    tm = min(t, 256)
    blk = pl.BlockSpec((tm, D), lambda i: (i, 0))
    return pl.pallas_call(
        _norm_kernel,
        out_shape=jax.ShapeDtypeStruct((t, D), out_dtype),
        grid=(t // tm,),
        in_specs=[blk, pl.BlockSpec((1, D), lambda i: (0, 0))],
        out_specs=blk,
        compiler_params=_params("parallel"),
        name="rmsnorm",
    )(h, w.reshape(1, D))


def _add_norm(h, o, w, out_dtype):
    t = h.shape[0]
    tm = min(t, 256)
    blk = pl.BlockSpec((tm, D), lambda i: (i, 0))
    return pl.pallas_call(
        _add_norm_kernel,
        out_shape=(jax.ShapeDtypeStruct((t, D), F32), jax.ShapeDtypeStruct((t, D), out_dtype)),
        grid=(t // tm,),
        in_specs=[blk, pl.BlockSpec((None, tm, D), lambda i: (0, i, 0)),
                  pl.BlockSpec((None, tm, D), lambda i: (1, i, 0)),
                  pl.BlockSpec((1, D), lambda i: (0, 0))],
        out_specs=(blk, blk),
        compiler_params=_params("parallel"),
        name="add_rmsnorm",
    )(h, o, o, w.reshape(1, D))


def _inproj_kernel(n_ref, wm_ref, ws_ref, wst_ref, um_ref, us_ref, ust_ref):
    x = n_ref[...]
    um_ref[...] = _dot(x, wm_ref[...]).astype(um_ref.dtype)

    @pl.when(pl.program_id(1) == 0)
    def _():
        us_ref[...] = _dot(x, ws_ref[...])
        ust_ref[...] = _dot_nt(wst_ref[...], x)


def _in_proj(n, w_main, w_small, w_small_t):
    t = n.shape[0]
    tm = min(t, 1024)
    tn = 11 * LANES
    assert IN_COLS % tn == 0
    return pl.pallas_call(
        _inproj_kernel,
        out_shape=(jax.ShapeDtypeStruct((t, IN_COLS), BF16),
                   jax.ShapeDtypeStruct((t, LANES), F32),
                   jax.ShapeDtypeStruct((LANES, t), F32)),
        grid=(t // tm, IN_COLS // tn),
        in_specs=[pl.BlockSpec((tm, D), lambda i, j: (i, 0)),
                  pl.BlockSpec((D, tn), lambda i, j: (0, j)),
                  pl.BlockSpec((D, LANES), lambda i, j: (0, 0)),
                  pl.BlockSpec((LANES, D), lambda i, j: (0, 0))],
        out_specs=(pl.BlockSpec((tm, tn), lambda i, j: (i, j)),
                   pl.BlockSpec((tm, LANES), lambda i, j: (i, 0)),
                   pl.BlockSpec((LANES, tm), lambda i, j: (0, i))),
        compiler_params=_params("parallel", "arbitrary"),
        name="in_proj",
    )(n, w_main, w_small, w_small_t)


KV_BLOCK = 256


def _kv_kernel(ckv_ref, nw_ref, wuk_ref, wuvt_ref, us_ref, cs_ref, sn_ref, k_ref, vt_ref, c_sc):
    hh = pl.program_id(1)

    @pl.when(hh == 0)
    def _():
        c_sc[...] = _rms(ckv_ref[...].astype(F32), nw_ref[...]).astype(BF16)

    c = c_sc[...]
    k_ref[:, :NOPE] = _dot(c, wuk_ref[...]).astype(k_ref.dtype)
    vt = _dot_nt(wuvt_ref[...], c).astype(vt_ref.dtype)
    for j in range(vt_ref.shape[0]):
        vt_ref[j] = vt[:, j * KV_BLOCK:(j + 1) * KV_BLOCK]
    u = us_ref[...]
    roped = u * cs_ref[...] + _rope_partner(u) * sn_ref[...]
    lane = _lane_iota(roped.shape)
    even = jnp.where(lane < ROPE, roped, 0.0)
    odd = jnp.where(lane >= ROPE, pltpu.roll(roped, ROPE, 1), 0.0)
    k_ref[:, NOPE:] = jnp.where(hh % 2 == 0, even, odd).astype(k_ref.dtype)


def _kv_proj(u_main, u_small, kv_norm_w, w_uk, w_uv_t, cs_tab, sn_tab, b, s):
    t = b * s
    tm = min(s, 1024)
    spt = s // tm
    nblk = tm // KV_BLOCK
    tab = pl.BlockSpec((tm, LANES), lambda i, h: (i, 0))
    return pl.pallas_call(
        _kv_kernel,
        out_shape=(jax.ShapeDtypeStruct((b, HEADS, s, 2 * LANES), BF16),
                   jax.ShapeDtypeStruct((b, HEADS, s // KV_BLOCK, VDIM, KV_BLOCK), BF16)),
        grid=(t // tm, HEADS),
        in_specs=[pl.BlockSpec((tm, KV_RANK), lambda i, h: (i, OFF_CKV // KV_RANK)),
                  pl.BlockSpec((1, KV_RANK), lambda i, h: (0, 0)),
                  pl.BlockSpec((KV_RANK, NOPE), lambda i, h: (0, h)),
                  pl.BlockSpec((VDIM, KV_RANK), lambda i, h: (h, 0)),
                  tab, tab, tab],
        out_specs=(pl.BlockSpec((None, None, tm, 2 * LANES), lambda i, h: (i // spt, h, i % spt, 0)),
                   pl.BlockSpec((None, None, nblk, VDIM, KV_BLOCK), lambda i, h: (i // spt, h, i % spt, 0, 0))),
        scratch_shapes=[pltpu.VMEM((tm, KV_RANK), BF16)],
        compiler_params=_params("parallel", "arbitrary"),
        name="kv_proj",
    )(u_main, kv_norm_w.reshape(1, KV_RANK), w_uk, w_uv_t, u_small, cs_tab, sn_tab)


Q_BLOCK = 256
Q_SUBBLOCKS = 4
KV_UNROLL = 2


def _sublane_all(x, op):
    for shift in (4, 2, 1):
        x = op(x, pltpu.roll(x, shift, 0))
    return x


def _attn_kernel(qn_ref, qr_ref, cs_ref, sn_ref, k_ref, vt_ref, o_ref, qt_sc, s_sc, m_sc, l_sc, acc_sc, *, scale):
    nq = m_sc.shape[0]
    nblk = vt_ref.shape[0]
    kgrp = KV_BLOCK // SUBLANES
    vgrp = VDIM // SUBLANES
    qr = qr_ref[...].astype(F32)
    roped = qr * cs_ref[...] + _rope_partner(qr) * sn_ref[...]
    q = jnp.concatenate([qn_ref[...].astype(F32), roped], axis=1) * scale
    qt_sc[...] = q.T.astype(BF16)
    m_sc[...] = jnp.full(m_sc.shape, -jnp.inf, F32)
    l_sc[...] = jnp.zeros(l_sc.shape, F32)
    acc_sc[...] = jnp.zeros(acc_sc.shape, F32)

    def scores(c, j):
        kc = k_ref[pl.ds(pl.multiple_of(c * KV_BLOCK, KV_BLOCK), KV_BLOCK), :]
        s_sc[j] = _dot(kc, qt_sc[:, j * Q_BLOCK:(j + 1) * Q_BLOCK])

    for j in range(nq):
        scores(0, j)

    def body(c, carry):
        nxt = jnp.minimum(c + 1, nblk - 1)
        vtc = vt_ref[c]
        for j in range(nq):
            sc = s_sc[j].reshape(kgrp, SUBLANES, Q_BLOCK)
            m_prev = m_sc[j]
            m_new = jnp.maximum(m_prev, _sublane_all(jnp.max(sc, axis=0), jnp.maximum))
            alpha = jnp.exp(m_prev - m_new)
            p = jnp.exp(sc - m_new[None])
            l_sc[j] = alpha * l_sc[j] + jnp.sum(p, axis=0)
            pv = _dot(vtc, p.reshape(KV_BLOCK, Q_BLOCK).astype(BF16))
            scores(nxt, j)
            acc = acc_sc[j].reshape(vgrp, SUBLANES, Q_BLOCK) * alpha[None]
            acc_sc[j] = acc.reshape(VDIM, Q_BLOCK) + pv
            m_sc[j] = m_new
        return carry

    lax.fori_loop(0, nblk, body, 0, unroll=min(KV_UNROLL, nblk))
    for j in range(nq):
        inv = 1.0 / _sublane_all(l_sc[j], jnp.add)
        out_t = (acc_sc[j].reshape(vgrp, SUBLANES, Q_BLOCK) * inv[None]).reshape(VDIM, Q_BLOCK)
        o_ref[j * Q_BLOCK:(j + 1) * Q_BLOCK, :] = out_t.T.astype(o_ref.dtype)


def _attention(u_main, k, vt, cs_tab, sn_tab, b, s):
    t = b * s
    tq = min(s, Q_SUBBLOCKS * Q_BLOCK)
    nq = tq // Q_BLOCK
    qpt = s // tq
    tab = pl.BlockSpec((tq, LANES), lambda bb, h, qi: (bb * qpt + qi, 0))
    kern = functools.partial(_attn_kernel, scale=float(NOPE + ROPE) ** -0.5)
    return pl.pallas_call(
        kern,
        out_shape=jax.ShapeDtypeStruct((t, HEADS * VDIM), BF16),
        grid=(b, HEADS, qpt),
        in_specs=[pl.BlockSpec((tq, NOPE), lambda bb, h, qi: (bb * qpt + qi, h)),
                  pl.BlockSpec((tq, LANES), lambda bb, h, qi: (bb * qpt + qi, OFF_QR // LANES + h // 2)),
                  tab, tab,
                  pl.BlockSpec((None, None, s, 2 * LANES), lambda bb, h, qi: (bb, h, 0, 0)),
                  pl.BlockSpec((None, None, s // KV_BLOCK, VDIM, KV_BLOCK), lambda bb, h, qi: (bb, h, 0, 0, 0))],
        out_specs=pl.BlockSpec((tq, VDIM), lambda bb, h, qi: (bb * qpt + qi, h)),
        scratch_shapes=[pltpu.VMEM((2 * LANES, tq), BF16), pltpu.VMEM((nq, KV_BLOCK, Q_BLOCK), F32),
                        pltpu.VMEM((nq, SUBLANES, Q_BLOCK), F32), pltpu.VMEM((nq, SUBLANES, Q_BLOCK), F32),
                        pltpu.VMEM((nq, VDIM, Q_BLOCK), F32)],
        compiler_params=_params("parallel", "parallel", "arbitrary"),
        name="mla_attention",
    )(u_main, u_main, cs_tab, sn_tab, k, vt)


HALO = BF16_ROWS
CONV_COLS = 512
assert all(o % CONV_COLS == 0 for o in (OFF_XBC, OFF_SCB, OFF_SCC, OFF_SCX))


def _fill_ext(ext_ref, prev, cur, nxt, first, last):
    tm = cur.shape[0]
    ext_ref[:HALO, :] = jnp.where(first, 0.0, prev.astype(F32))
    ext_ref[HALO:HALO + tm, :] = cur.astype(F32)
    ext_ref[HALO + tm:, :] = jnp.where(last, 0.0, nxt.astype(F32))


def _taps(ext_ref, w_ref, ksize, tm):
    pad = ksize // 2
    acc = None
    for kk in range(ksize):
        term = ext_ref[pl.ds(HALO - pad + kk, tm), :] * w_ref[kk:kk + 1, :]
        acc = term if acc is None else acc + term
    return acc


def _ssm_conv_kernel(prev_ref, cur_ref, next_ref, w_ref, b_ref, o_ref, ext_ref, *, tiles_per_seq):
    i = pl.program_id(0)
    tm = cur_ref.shape[0]
    _fill_ext(ext_ref, prev_ref[...], cur_ref[...], next_ref[...],
              i % tiles_per_seq == 0, i % tiles_per_seq == tiles_per_seq - 1)
    o_ref[...] = _silu(_taps(ext_ref, w_ref, SSM_CONV, tm) + b_ref[...]).astype(o_ref.dtype)


def _halo_specs(tm, tc, col0, n_rows):
    r = tm // HALO
    last = n_rows // HALO - 1
    cb = col0 // tc
    return [pl.BlockSpec((HALO, tc), lambda i, j: (jnp.maximum(i * r - 1, 0), cb + j)),
            pl.BlockSpec((tm, tc), lambda i, j: (i, cb + j)),
            pl.BlockSpec((HALO, tc), lambda i, j: (jnp.minimum((i + 1) * r, last), cb + j))]


def _ssm_conv(u_main, conv_w, conv_b, s):
    t = u_main.shape[0]
    tm = min(s, 512)
    tc = CONV_COLS
    kern = functools.partial(_ssm_conv_kernel, tiles_per_seq=s // tm)
    return pl.pallas_call(
        kern,
        out_shape=jax.ShapeDtypeStruct((t, XBC), BF16),
        grid=(t // tm, XBC // tc),
        in_specs=_halo_specs(tm, tc, OFF_XBC, t) + [
            pl.BlockSpec((SSM_CONV, tc), lambda i, j: (0, j)),
            pl.BlockSpec((1, tc), lambda i, j: (0, j))],
        out_specs=pl.BlockSpec((tm, tc), lambda i, j: (i, j)),
        scratch_shapes=[pltpu.VMEM((tm + 2 * HALO, tc), F32)],
        compiler_params=_params("parallel", "parallel"),
        name="ssm_conv",
    )(u_main, u_main, u_main, conv_w, conv_b.reshape(1, XBC))


def _sconv_kernel(cp_ref, cc_ref, cn_ref, xp_ref, xc_ref, xn_ref, b_ref, w_ref, o_ref, ext_ref, *, tiles_per_seq):
    i = pl.program_id(0)
    tm = cc_ref.shape[0]
    f = lambda r: r[...].astype(F32)
    _fill_ext(ext_ref, f(cp_ref) * f(xp_ref), f(cc_ref) * f(xc_ref), f(cn_ref) * f(xn_ref),
              i % tiles_per_seq == 0, i % tiles_per_seq == tiles_per_seq - 1)
    o_ref[...] = (f(b_ref) * _taps(ext_ref, w_ref, SCONV_K, tm)).astype(o_ref.dtype)


def _short_conv(u_main, sconv_w, s):
    t = u_main.shape[0]
    tm = min(s, 512)
    tc = CONV_COLS
    kern = functools.partial(_sconv_kernel, tiles_per_seq=s // tm)
    return pl.pallas_call(
        kern,
        out_shape=jax.ShapeDtypeStruct((t, D), BF16),
        grid=(t // tm, D // tc),
        in_specs=_halo_specs(tm, tc, OFF_SCC, t) + _halo_specs(tm, tc, OFF_SCX, t) + [
            pl.BlockSpec((tm, tc), lambda i, j: (i, OFF_SCB // tc + j)),
            pl.BlockSpec((SCONV_K, tc), lambda i, j: (0, j))],
        out_specs=pl.BlockSpec((tm, tc), lambda i, j: (i, j)),
        scratch_shapes=[pltpu.VMEM((tm + 2 * HALO, tc), F32)],
        compiler_params=_params("parallel", "parallel"),
        name="short_conv",
    )(u_main, u_main, u_main, u_main, u_main, u_main, u_main, sconv_w)


FWD_COL = 2 * SSM_HEADS
BWD_COL = 3 * SSM_HEADS
PAIRS = SSM_HEADS // 2
PAIRS_PER_GROUP = PAIRS // SSM_GROUPS


def _cumsum(x, axis, reverse):
    n = x.shape[axis]
    idx = lax.broadcasted_iota(I32, x.shape, axis)
    k = 1
    while k < n:
        if reverse:
            x = x + jnp.where(idx < n - k, pltpu.roll(x, n - k, axis), 0.0)
        else:
            x = x + jnp.where(idx >= k, pltpu.roll(x, k, axis), 0.0)
        k *= 2
    return x


def _ssd_direction(xbc_ref, us_ref, ust_ref, dtb_row, a_row, dtb_col, a_col, d_row, y_ref, st_ref, *, reverse):
    base = BWD_COL if reverse else FWD_COL
    edge = 0 if reverse else CHUNK - 1
    dt_c = _softplus(us_ref[...] + dtb_row)
    cum_c = _cumsum(dt_c * a_row, 0, reverse)
    dt_r = _softplus(ust_ref[...] + dtb_col)
    cum_r = _cumsum(dt_r * a_col, 1, reverse)
    ecum_c = jnp.exp(cum_c)
    tot = cum_c[edge:edge + 1, :]
    dec_c = jnp.exp(tot - cum_c) * dt_c
    etot = jnp.exp(tot)

    li = lax.broadcasted_iota(I32, (CHUNK, CHUNK), 0)
    si = lax.broadcasted_iota(I32, (CHUNK, CHUNK), 1)
    mask = (li <= si) if reverse else (li >= si)
    lane = _lane_iota((CHUNK, LANES))
    lo = lane < SSM_P

    for g in range(SSM_GROUPS):
        bg = xbc_ref[:, SSM_INNER + g * SSM_N:SSM_INNER + (g + 1) * SSM_N]
        cg = xbc_ref[:, SSM_INNER + (SSM_GROUPS + g) * SSM_N:SSM_INNER + (SSM_GROUPS + g + 1) * SSM_N]
        cb = _dot_nt(cg, bg)
        for j in range(PAIRS_PER_GROUP):
            pr = g * PAIRS_PER_GROUP + j
            c1, c2 = base + 2 * pr, base + 2 * pr + 1
            cols = slice(pr * LANES, (pr + 1) * LANES)

            def pair(arr):
                return jnp.where(lo, arr[:, c1:c1 + 1], arr[:, c2:c2 + 1])

            xs = xbc_ref[:, cols].astype(F32)
            xdt = xs * pair(dt_c)
            scores = []
            for c in (c1, c2):
                seg = cum_c[:, c:c + 1] - cum_r[c:c + 1, :]
                scores.append((cb * jnp.exp(jnp.where(mask, seg, -jnp.inf))).astype(BF16))
            xb = xdt.astype(BF16)
            zero = jnp.zeros_like(xb)
            x_bd = jnp.concatenate([jnp.where(lo, xb, zero), jnp.where(lo, zero, xb)], axis=0)
            y = _dot(jnp.concatenate(scores, axis=1), x_bd)
            st = st_ref[g, :, j * LANES:(j + 1) * LANES]
            y = y + _dot(cg, st.astype(BF16)) * pair(ecum_c)
            if not reverse:
                y = y + xs * d_row[:, cols]
            y_ref[:, cols] = y.astype(y_ref.dtype)
            new = _dot_tn(bg, (xs * pair(dec_c)).astype(BF16))
            st_ref[g, :, j * LANES:(j + 1) * LANES] = st * pair(etot) + new


def _ssd_kernel(xf_ref, usf_ref, ustf_ref, xb_ref, usb_ref, ustb_ref, dtb_row_ref, alog_row_ref,
                dtb_col_ref, alog_col_ref, d_ref, yf_ref, yb_ref, stf_ref, stb_ref):
    @pl.when(pl.program_id(1) == 0)
    def _():
        stf_ref[...] = jnp.zeros(stf_ref.shape, F32)
        stb_ref[...] = jnp.zeros(stb_ref.shape, F32)

    dtb_row, dtb_col = dtb_row_ref[...], dtb_col_ref[...]
    a_row, a_col = -jnp.exp(alog_row_ref[...]), -jnp.exp(alog_col_ref[...])
    d_row = d_ref[...]
    _ssd_direction(xf_ref, usf_ref, ustf_ref, dtb_row, a_row, dtb_col, a_col, d_row, yf_ref, stf_ref, reverse=False)
    _ssd_direction(xb_ref, usb_ref, ustb_ref, dtb_row, a_row, dtb_col, a_col, d_row, yb_ref, stb_ref, reverse=True)


def _ssd_scan(xbc_conv, u_small, u_small_t, a_log, dt_bias, d_skip, b, s):
    t = b * s
    nc = s // CHUNK
    pad = jnp.zeros((2 * SSM_HEADS,), F32)
    dtb = jnp.concatenate([pad, dt_bias.reshape(-1)])
    alog = jnp.concatenate([pad, a_log.reshape(-1)])
    d_row = jnp.repeat(d_skip, SSM_P)[None, :]
    fw = lambda bb, i: bb * nc + i
    bw = lambda bb, i: bb * nc + nc - 1 - i
    row = lambda w: pl.BlockSpec((1, w), lambda bb, i: (0, 0))
    col = pl.BlockSpec((LANES, 1), lambda bb, i: (0, 0))

    def chunk_specs(idx):
        return [pl.BlockSpec((CHUNK, XBC), lambda bb, i: (idx(bb, i), 0)),
                pl.BlockSpec((CHUNK, LANES), lambda bb, i: (idx(bb, i), 0)),
                pl.BlockSpec((LANES, CHUNK), lambda bb, i: (0, idx(bb, i)))]

    return pl.pallas_call(
        _ssd_kernel,
        out_shape=(jax.ShapeDtypeStruct((t, SSM_INNER), BF16),) * 2,
        grid=(b, nc),
        in_specs=chunk_specs(fw) + chunk_specs(bw) + [row(LANES), row(LANES), col, col, row(SSM_INNER)],
        out_specs=(pl.BlockSpec((CHUNK, SSM_INNER), lambda bb, i: (fw(bb, i), 0)),
                   pl.BlockSpec((CHUNK, SSM_INNER), lambda bb, i: (bw(bb, i), 0))),
        scratch_shapes=[pltpu.VMEM((SSM_GROUPS, SSM_N, SSM_INNER // SSM_GROUPS), F32)] * 2,
        compiler_params=_params("parallel", "arbitrary"),
        name="ssd_scan",
    )(xbc_conv, u_small, u_small_t, xbc_conv, u_small, u_small_t,
      dtb[None, :], alog[None, :], dtb[:, None], alog[:, None], d_row)


def _ssm_post_kernel(yf_ref, yb_ref, z_ref, w_ref, o_ref):
    y = (yf_ref[...].astype(F32) + yb_ref[...].astype(F32)) * _silu(z_ref[...].astype(F32))
    o_ref[...] = _rms(y, w_ref[...]).astype(o_ref.dtype)


def _ssm_post(y_fwd, y_bwd, u_main, norm_w):
    t = y_fwd.shape[0]
    tm = min(t, 512)
    gw = SSM_INNER // SSM_GROUPS
    blk = pl.BlockSpec((tm, gw), lambda i, g: (i, g))
    return pl.pallas_call(
        _ssm_post_kernel,
        out_shape=jax.ShapeDtypeStruct((t, SSM_INNER), BF16),
        grid=(t // tm, SSM_GROUPS),
        in_specs=[blk, blk, pl.BlockSpec((tm, gw), lambda i, g: (i, OFF_Z // gw + g)),
                  pl.BlockSpec((1, gw), lambda i, g: (0, g))],
        out_specs=blk,
        compiler_params=_params("parallel", "parallel"),
        name="ssm_gate_norm",
    )(y_fwd, y_bwd, u_main, norm_w.reshape(1, SSM_INNER))


def _merge_kernel(ya_ref, ys_ref, yc_ref, pa_ref, ps_ref, pc_ref, ga_ref, gs_ref, gc_ref, o_ref):
    g = lambda r: _sigmoid(r[...].astype(F32))
    m = (g(ga_ref) * _dot(ya_ref[...], pa_ref[...]) + g(gs_ref) * _dot(ys_ref[...], ps_ref[...])
         + g(gc_ref) * _dot(yc_ref[...], pc_ref[...]))
    o_ref[...] = m.astype(o_ref.dtype)


def _merge(y_att, y_ssm, y_conv, w_branch, u_main):
    t = y_att.shape[0]
    tm = min(t, 512)
    tn = 512
    yb = pl.BlockSpec((tm, D), lambda i, j: (i, 0))
    pspec = lambda r: pl.BlockSpec((D, tn), lambda i, j: (r, j))
    gspec = lambda r: pl.BlockSpec((tm, tn), lambda i, j: (i, OFF_GATE // tn + r * (D // tn) + j))
    return pl.pallas_call(
        _merge_kernel,
        out_shape=jax.ShapeDtypeStruct((t, D), BF16),
        grid=(t // tm, D // tn),
        in_specs=[yb, yb, yb, pspec(0), pspec(1), pspec(2), gspec(0), gspec(1), gspec(2)],
        out_specs=pl.BlockSpec((tm, tn), lambda i, j: (i, j)),
        compiler_params=_params("parallel", "arbitrary"),
        name="branch_merge",
    )(y_att, y_ssm, y_conv, w_branch, w_branch, w_branch, u_main, u_main, u_main)


def _split_bf16(x):
    hi = x.astype(BF16)
    lo = (x - hi.astype(F32)).astype(BF16)
    return hi, lo


def _wo_kernel(m_ref, h_ref, wo_ref, nw_ref, rhi_ref, rlo_ref, rb_ref, h1_ref, t_ref, lg_ref):
    h1 = h_ref[...] + _dot(m_ref[...], wo_ref[...])
    h1_ref[...] = h1
    tn = _rms(h1, nw_ref[...])
    t_ref[...] = tn
    hi, lo = _split_bf16(tn)
    lg_ref[...] = _dot(hi, rhi_ref[...]) + _dot(lo, rhi_ref[...]) + _dot(hi, rlo_ref[...]) + rb_ref[...]


def _out_proj(merged, h, w_o, ffn_norm_w, r_hi, r_lo, r_bias):
    t = h.shape[0]
    tm = min(t, 256)
    blk = pl.BlockSpec((tm, D), lambda i: (i, 0))
    const = lambda shape: pl.BlockSpec(shape, lambda i: (0, 0))
    lg = pl.BlockSpec((tm, LANES), lambda i: (i, 0))
    return pl.pallas_call(
        _wo_kernel,
        out_shape=(jax.ShapeDtypeStruct((t, D), F32), jax.ShapeDtypeStruct((t, D), F32),
                   jax.ShapeDtypeStruct((t, LANES), F32)),
        grid=(t // tm,),
        in_specs=[blk, blk, const((D, D)), const((1, D)), const((D, LANES)), const((D, LANES)),
                  const((1, LANES))],
        out_specs=(blk, blk, lg),
        compiler_params=_params("parallel"),
        name="out_proj_router",
    )(merged, h, w_o, ffn_norm_w.reshape(1, D), r_hi, r_lo, r_bias)


def _route_kernel(lg_ref, id_ref, wt_ref):
    lg = lg_ref[...]
    lane = _lane_iota(lg.shape).astype(F32)
    big = float(LANES)
    ninf = -jnp.inf
    first = lambda hit: jnp.min(jnp.where(hit, lane, big), axis=1, keepdims=True)

    gl = jnp.where(lane < N_GROUPS, lg, ninf)
    gmax = jnp.max(gl, axis=1, keepdims=True)
    gidx = first(gl == gmax)
    g_w = 1.0 / jnp.sum(jnp.exp(gl - gmax), axis=1, keepdims=True)

    in_group = (lane >= N_GROUPS + gidx * EPG) & (lane < N_GROUPS + (gidx + 1.0) * EPG)
    el = jnp.where(in_group, lg, ninf)
    v0 = jnp.max(el, axis=1, keepdims=True)
    i0 = first(el == v0)
    el = jnp.where(lane == i0, ninf, el)
    v1 = jnp.max(el, axis=1, keepdims=True)
    i1 = first(el == v1)
    e1 = jnp.exp(v1 - v0)
    w0 = g_w / (1.0 + e1)
    w1 = g_w * e1 / (1.0 + e1)
    id_ref[...] = jnp.where(lane == 0.0, i0, i1).astype(I32) - N_GROUPS
    wt_ref[...] = jnp.where(lane == 0.0, w0, w1)


def _route(logits):
    t = logits.shape[0]
    tm = min(t, 1024)
    blk = pl.BlockSpec((tm, LANES), lambda i: (i, 0))
    return pl.pallas_call(
        _route_kernel,
        out_shape=(jax.ShapeDtypeStruct((t, LANES), I32), jax.ShapeDtypeStruct((t, LANES), F32)),
        grid=(t // tm,),
        in_specs=[blk],
        out_specs=(blk, blk),
        compiler_params=_params("parallel"),
        name="route",
    )(logits)


def _dispatch_plan(ids, wts, t, tile):
    n_rows = 2 * t
    n_tiles = n_rows // tile + N_EXPERTS
    n_pos = n_tiles * tile
    e = ids[:, :2].reshape(-1)
    w = wts[:, :2].reshape(-1)
    onehot = (e[:, None] == jnp.arange(N_EXPERTS, dtype=I32)[None, :]).astype(I32)
    csum = jnp.cumsum(onehot, axis=0)
    rank = jnp.take_along_axis(csum, e[:, None], axis=1)[:, 0] - 1
    counts = csum[-1]
    padded = ((counts + tile - 1) // tile) * tile
    ends = jnp.cumsum(padded)
    starts = ends - padded
    pos = starts[e] + rank
    r = jnp.arange(n_rows, dtype=I32)
    packed = jnp.stack([r, lax.bitcast_convert_type(w, I32)], axis=1)
    slots = jnp.full((n_pos, 2), -1, I32).at[pos].set(packed)
    src = slots[:, 0]
    valid = src >= 0
    tile_start = jnp.arange(n_tiles, dtype=I32) * tile
    tile_e = jnp.minimum(jnp.searchsorted(ends, tile_start, side="right"), N_EXPERTS - 1).astype(I32)
    tile_rows = jnp.clip((starts + counts)[tile_e] - tile_start, 0, tile).astype(I32)
    row_tok = jnp.where(valid, src // 2, 0)
    row_dst = jnp.where(valid, (src % 2) * t + src // 2, 0)
    row_w = jnp.where(valid, lax.bitcast_convert_type(slots[:, 1], F32), 0.0)
    n_used = (ends[-1] // tile).astype(I32).reshape(1)
    return (row_tok.reshape(n_tiles, 1, tile), row_dst.reshape(n_tiles, 1, tile),
            row_w.reshape(n_pos, 1), tile_e, tile_rows, n_used)


DMA_UNROLL = 8


def _moe_kernel(te_ref, tr_ref, nu_ref, tok_ref, tokn_ref, dst_ref, rw_ref, t_hbm, wg_ref, wu_ref, wd_ref, o_hbm,
                xbuf, ybuf, wg_sc, wu_sc, wd_sc, gsem, ssem):
    i = pl.program_id(0)
    n_used = nu_ref[0]
    tile = rw_ref.shape[0]
    rows = tile * PITCH
    slot = i % 2

    def start_gather(tok, slot_):
        base = slot_ * rows

        def body(r, c):
            dst = xbuf.at[pl.ds(pl.multiple_of(base + r * PITCH, SUBLANES), SLAB), :]
            pltpu.make_async_copy(t_hbm.at[tok[0, 0, r]], dst, gsem.at[slot_]).start()
            return c

        lax.fori_loop(0, tile, body, 0, unroll=DMA_UNROLL)

    def wait_rows(buf, n_rows, sem):
        done = buf.at[pl.ds(0, n_rows * SLAB), :]
        pltpu.make_async_copy(done, done, sem).wait()

    @pl.when(i == 0)
    def _():
        start_gather(tok_ref, 0)

    @pl.when(i < n_used)
    def _():
        @pl.when((i == 0) | (te_ref[i] != te_ref[jnp.maximum(i - 1, 0)]))
        def _():
            wg_sc[...] = wg_ref[...].astype(BF16)
            wu_sc[...] = wu_ref[...].astype(BF16)
            wd_sc[...] = wd_ref[...].astype(BF16)

        @pl.when(i + 1 < n_used)
        def _():
            start_gather(tokn_ref, 1 - slot)

        wait_rows(xbuf, tile, gsem.at[slot])
        base = slot * rows
        x = jnp.concatenate([xbuf[pl.ds(base + sl, tile, stride=PITCH), :] for sl in range(SLAB)],
                            axis=1).astype(BF16)
        gate = _dot(x, wg_sc[...])
        mid = (_silu(gate) * _dot(x, wu_sc[...])).astype(BF16)
        y = _dot(mid, wd_sc[...]) * rw_ref[...]

        @pl.when(i > 0)
        def _():
            wait_rows(ybuf, tr_ref[jnp.maximum(i - 1, 0)], ssem)

        for sl in range(SLAB):
            ybuf[pl.ds(sl, tile, stride=PITCH), :] = y[:, sl * LANES:(sl + 1) * LANES]

        n_real = tr_ref[i]

        def s_body(r, c):
            @pl.when(r < n_real)
            def _():
                src = ybuf.at[pl.ds(pl.multiple_of(r * PITCH, SUBLANES), SLAB), :]
                pltpu.make_async_copy(src, o_hbm.at[dst_ref[0, 0, r]], ssem).start()
            return c

        lax.fori_loop(0, tile, s_body, 0, unroll=DMA_UNROLL)

        @pl.when(i == n_used - 1)
        def _():
            wait_rows(ybuf, n_real, ssem)


def _moe(t_norm, plan, w_gate, w_up, w_down):
    t = t_norm.shape[0]
    row_tok, row_dst, row_w, tile_e, tile_rows, n_used = plan
    n_tiles, _, tile = row_tok.shape
    t_slab = t_norm.reshape(t, SLAB, LANES)
    smem = pl.BlockSpec((1, 1, tile), lambda i, te, tr, nu: (i, 0, 0), memory_space=pltpu.SMEM)
    smem_next = pl.BlockSpec((1, 1, tile), lambda i, te, tr, nu: (jnp.minimum(i + 1, n_tiles - 1), 0, 0),
                             memory_space=pltpu.SMEM)
    wspec = lambda a, b: pl.BlockSpec((None, a, b), lambda i, te, tr, nu: (te[i], 0, 0))
    out = pl.pallas_call(
        _moe_kernel,
        out_shape=jax.ShapeDtypeStruct((2 * t, SLAB, LANES), F32),
        grid_spec=pltpu.PrefetchScalarGridSpec(
            num_scalar_prefetch=3,
            grid=(n_tiles,),
            in_specs=[smem, smem_next, smem,
                      pl.BlockSpec((tile, 1), lambda i, te, tr, nu: (i, 0)),
                      pl.BlockSpec(memory_space=pl.ANY),
                      wspec(D, D_EXPERT), wspec(D, D_EXPERT), wspec(D_EXPERT, D)],
            out_specs=pl.BlockSpec(memory_space=pl.ANY),
            scratch_shapes=[pltpu.VMEM((2 * tile * PITCH, LANES), F32), pltpu.VMEM((tile * PITCH, LANES), F32),
                            pltpu.VMEM((D, D_EXPERT), BF16), pltpu.VMEM((D, D_EXPERT), BF16),
                            pltpu.VMEM((D_EXPERT, D), BF16),
                            pltpu.SemaphoreType.DMA((2,)), pltpu.SemaphoreType.DMA(())]),
        compiler_params=_params("arbitrary"),
        name="moe_experts",
    )(tile_e, tile_rows, n_used, row_tok, row_tok, row_dst, row_w, t_slab, w_gate, w_up, w_down)
    return out.reshape(2, t, D)


def kernel(x, positions, attn_norm_w, w_in, kv_norm_w, w_uk, w_uv, ssm_conv_w, ssm_conv_b, ssm_A_log, ssm_dt_bias, ssm_D, ssm_norm_w, sconv_w, w_branch, w_o, ffn_norm_w, router_group_w, router_group_b, router_expert_w, router_expert_b, expert_w_gate, expert_w_up, expert_w_down, final_norm_w):
    b, s, _ = x.shape
    t = b * s
    depth = w_in.shape[0]
    h = x.reshape(t, D)
    cs_tab, sn_tab = _rope_tables(positions)
    moe_out = None
    for l in range(depth):
        if moe_out is None:
            n = _norm(h, attn_norm_w[l], BF16)
        else:
            h, n = _add_norm(h, moe_out, attn_norm_w[l], BF16)
        w_perm = _permute_in_proj(w_in[l])
        w_small = w_perm[:, OFF_SMALL:]
        u_main, u_small, u_small_t = _in_proj(n, w_perm, w_small, w_small.T)

        k, vt = _kv_proj(u_main, u_small, kv_norm_w[l], w_uk[l].astype(BF16), w_uv[l].T.astype(BF16),
                         cs_tab, sn_tab, b, s)
        y_att = _attention(u_main, k, vt, cs_tab, sn_tab, b, s)

        xbc_conv = _ssm_conv(u_main, ssm_conv_w[l], ssm_conv_b[l], s)
        y_fwd, y_bwd = _ssd_scan(xbc_conv, u_small, u_small_t, ssm_A_log[l], ssm_dt_bias[l], ssm_D[l], b, s)
        y_ssm = _ssm_post(y_fwd, y_bwd, u_main, ssm_norm_w[l])

        y_conv = _short_conv(u_main, sconv_w[l], s)

        merged = _merge(y_att, y_ssm, y_conv, w_branch[l].astype(BF16), u_main)

        r_w = jnp.concatenate([router_group_w[l], router_expert_w[l],
                               jnp.zeros((D, LANES - N_GROUPS - N_EXPERTS), F32)], axis=1)
        r_b = jnp.concatenate([router_group_b[l], router_expert_b[l],
                               jnp.zeros((LANES - N_GROUPS - N_EXPERTS,), F32)])[None, :]
        r_hi, r_lo = _split_bf16(r_w)
        h, t_norm, logits = _out_proj(merged, h, w_o[l].astype(BF16), ffn_norm_w[l], r_hi, r_lo, r_b)

        ids, wts = _route(logits)
        plan = _dispatch_plan(ids, wts, t, MOE_TILE)
        moe_out = _moe(t_norm, plan, expert_w_gate[l], expert_w_up[l], expert_w_down[l])

    h, out = _add_norm(h, moe_out, final_norm_w, F32)
    return out.reshape(b, s, D)
```

```python
import functools

import numpy as np
import jax
import jax.numpy as jnp
from jax import lax
from jax.experimental import pallas as pl
from jax.experimental.pallas import tpu as pltpu

F32 = jnp.float32
BF16 = jnp.bfloat16
I32 = jnp.int32

D = 2048
EPS = 1e-6
LOG2_E = 1.4426950408889634
HEADS = 16
NOPE = 128
ROPE = 64
VDIM = 128
KV_RANK = 512
ROPE_THETA = 10000.0
SSM_HEADS = 32
SSM_P = 64
SSM_INNER = SSM_HEADS * SSM_P
SSM_GROUPS = 4
SSM_N = 128
SSM_CONV = 5
CHUNK = 128
XBC = SSM_INNER + 2 * SSM_GROUPS * SSM_N
SCONV_K = 3
N_GROUPS = 8
EPG = 4
N_EXPERTS = N_GROUPS * EPG
D_EXPERT = 512
Q_DIM = HEADS * (NOPE + ROPE)
IN_COLS = Q_DIM + KV_RANK + ROPE + SSM_INNER + XBC + 2 * SSM_HEADS + 3 * D + 3 * D

LANES = 128
SUBLANES = 8
BF16_ROWS = 16
V7X_VMEM_BYTES = 64 * 1024 * 1024
VMEM_LIMIT = 56 * 1024 * 1024

OFF_QN = 0
OFF_QR = OFF_QN + HEADS * NOPE
OFF_CKV = OFF_QR + HEADS * ROPE
OFF_Z = OFF_CKV + KV_RANK
OFF_XBC = OFF_Z + SSM_INNER
OFF_SCB = OFF_XBC + XBC
OFF_SCC = OFF_SCB + D
OFF_SCX = OFF_SCC + D
OFF_GATE = OFF_SCX + D
OFF_SMALL = OFF_GATE + 3 * D
assert OFF_SMALL + LANES == IN_COLS
IN_PROJ_TN = 1024
U_COLS = -(-IN_COLS // IN_PROJ_TN) * IN_PROJ_TN

SLAB = D // LANES
PITCH = 24
MOE_TILE = 256


def _permute_in_proj(w):
    o_ckv = Q_DIM
    o_kr = o_ckv + KV_RANK
    o_z = o_kr + ROPE
    o_dt = o_z + SSM_INNER + XBC
    o_rest = o_dt + 2 * SSM_HEADS
    q = w[:, :Q_DIM].reshape(D, HEADS, NOPE + ROPE)
    parts = [q[:, :, :NOPE].reshape(D, HEADS * NOPE), q[:, :, NOPE:].reshape(D, HEADS * ROPE),
             w[:, o_ckv:o_kr], w[:, o_z:o_dt], w[:, o_rest:], w[:, o_kr:o_z], w[:, o_dt:o_rest]]
    parts.append(jnp.zeros((D, U_COLS - IN_COLS), w.dtype))
    out = jnp.concatenate([p.astype(BF16) for p in parts], axis=1)
    assert out.shape == (D, U_COLS)
    return out


def _params(*sem):
    return pltpu.CompilerParams(dimension_semantics=tuple(sem), vmem_limit_bytes=VMEM_LIMIT)


def _dot(a, b):
    return jnp.dot(a, b, preferred_element_type=F32)


def _dot_nt(a, b):
    return lax.dot_general(a, b, (((1,), (1,)), ((), ())), preferred_element_type=F32)


def _dot_tn(a, b):
    return lax.dot_general(a, b, (((0,), (0,)), ((), ())), preferred_element_type=F32)


def _sigmoid(x):
    return 1.0 / (1.0 + jnp.exp(-x))


def _silu(x):
    return x * _sigmoid(x)


def _softplus(x):
    return jnp.maximum(x, 0.0) + jnp.log(1.0 + jnp.exp(-jnp.abs(x)))


def _lane_iota(shape):
    return lax.broadcasted_iota(I32, shape, len(shape) - 1)


def _rope_partner(v):
    lane = _lane_iota(v.shape)
    return jnp.where((lane & 32) == 0, pltpu.roll(v, LANES - 32, 1), pltpu.roll(v, 32, 1))


def _rope_tab_kernel(pos_ref, freq_ref, sign_ref, cs_ref, sn_ref):
    ang = pos_ref[...].astype(F32) * freq_ref[...]
    cs_ref[...] = jnp.cos(ang)
    sn_ref[...] = jnp.sin(ang) * sign_ref[...]


def _rope_tables(positions):
    t = positions.size
    tm = min(t, 1024)
    freqs = ROPE_THETA ** (-jnp.arange(0, ROPE, 2, dtype=F32) / ROPE)
    freq_row = jnp.tile(freqs, 4)[None, :]
    sign_row = jnp.tile(jnp.concatenate([-jnp.ones(32, F32), jnp.ones(32, F32)]), 2)[None, :]
    row = pl.BlockSpec((1, LANES), lambda i: (0, 0))
    tab = pl.BlockSpec((tm, LANES), lambda i: (i, 0))
    return pl.pallas_call(
        _rope_tab_kernel,
        out_shape=(jax.ShapeDtypeStruct((t, LANES), F32),) * 2,
        grid=(t // tm,),
        in_specs=[pl.BlockSpec((tm, 1), lambda i: (i, 0)), row, row],
        out_specs=(tab, tab),
        compiler_params=_params("parallel"),
        name="rope_tables",
    )(positions.reshape(t, 1), freq_row, sign_row)


def _rms(x, w):
    return x * lax.rsqrt(jnp.mean(x * x, axis=-1, keepdims=True) + EPS) * w


def _norm_kernel(h_ref, w_ref, n_ref):
    n_ref[...] = _rms(h_ref[...], w_ref[...]).astype(n_ref.dtype)


def _rows_from_slabs(ref, tm):
    return jnp.concatenate([ref[pl.ds(sl, tm, stride=SLAB), :] for sl in range(SLAB)], axis=1)


def _add_norm_kernel(h_ref, o0_ref, o1_ref, w_ref, hs_ref, n_ref):
    tm = h_ref.shape[0]
    hs = h_ref[...] + _rows_from_slabs(o0_ref, tm) + _rows_from_slabs(o1_ref, tm)
    hs_ref[...] = hs
    n_ref[...] = _rms(hs, w_ref[...]).astype(n_ref.dtype)


def _norm(h, w, out_dtype):
    t = h.shape[0]
    tm = min(t, 256)
    blk = pl.BlockSpec((tm, D), lambda i: (i, 0))
    return pl.pallas_call(
        _norm_kernel,
        out_shape=jax.ShapeDtypeStruct((t, D), out_dtype),
        grid=(t // tm,),
        in_specs=[blk, pl.BlockSpec((1, D), lambda i: (0, 0))],
        out_specs=blk,
        compiler_params=_params("parallel"),
        name="rmsnorm",
    )(h, w.reshape(1, D))


def _add_norm(h, o, w, out_dtype):
    t = h.shape[0]
    tm = min(t, 256)
    nt = t // tm
    blk = pl.BlockSpec((tm, D), lambda i: (i, 0))
    return pl.pallas_call(
        _add_norm_kernel,
        out_shape=(jax.ShapeDtypeStruct((t, D), F32), jax.ShapeDtypeStruct((t, D), out_dtype)),
        grid=(nt,),
        in_specs=[blk, pl.BlockSpec((tm * SLAB, LANES), lambda i: (i, 0)),
                  pl.BlockSpec((tm * SLAB, LANES), lambda i: (nt + i, 0)),
                  pl.BlockSpec((1, D), lambda i: (0, 0))],
        out_specs=(blk, blk),
        compiler_params=_params("parallel"),
        name="add_rmsnorm",
    )(h, o, o, w.reshape(1, D))


def _inproj_kernel(n_ref, wm_ref, ws_ref, wst_ref, um_ref, us_ref, ust_ref):
    x = n_ref[...]
    um_ref[...] = _dot(x, wm_ref[...]).astype(um_ref.dtype)

    @pl.when(pl.program_id(1) == 0)
    def _():
        us_ref[...] = _dot(x, ws_ref[...])
        ust_ref[...] = _dot_nt(wst_ref[...], x)


def _in_proj(n, w_main, w_small, w_small_t):
    t = n.shape[0]
    tm = min(t, 1024)
    tn = IN_PROJ_TN
    return pl.pallas_call(
        _inproj_kernel,
        out_shape=(jax.ShapeDtypeStruct((t, U_COLS), BF16),
                   jax.ShapeDtypeStruct((t, LANES), F32),
                   jax.ShapeDtypeStruct((LANES, t), F32)),
        grid=(t // tm, U_COLS // tn),
        in_specs=[pl.BlockSpec((tm, D), lambda i, j: (i, 0)),
                  pl.BlockSpec((D, tn), lambda i, j: (0, j)),
                  pl.BlockSpec((D, LANES), lambda i, j: (0, 0)),
                  pl.BlockSpec((LANES, D), lambda i, j: (0, 0))],
        out_specs=(pl.BlockSpec((tm, tn), lambda i, j: (i, j)),
                   pl.BlockSpec((tm, LANES), lambda i, j: (i, 0)),
                   pl.BlockSpec((LANES, tm), lambda i, j: (0, i))),
        compiler_params=_params("parallel", "arbitrary"),
        name="in_proj",
    )(n, w_main, w_small, w_small_t)


KV_BLOCK = 1024


def _kv_kernel(ckv_ref, nw_ref, wuk_ref, wuvt_ref, us_ref, cs_ref, sn_ref, k_ref, vt_ref):
    nblk, kvb = vt_ref.shape[1], vt_ref.shape[3]
    c = _rms(ckv_ref[...].astype(F32), nw_ref[...]).astype(BF16)
    kn = _dot(c, wuk_ref[...]).astype(k_ref.dtype)
    vt = _dot_nt(wuvt_ref[...], c).astype(vt_ref.dtype)
    u = us_ref[...]
    roped = u * cs_ref[...] + _rope_partner(u) * sn_ref[...]
    lane = _lane_iota(roped.shape)
    even = jnp.where(lane < ROPE, roped, 0.0).astype(k_ref.dtype)
    odd = jnp.where(lane >= ROPE, pltpu.roll(roped, ROPE, 1), 0.0).astype(k_ref.dtype)
    for hh in range(HEADS):
        k_ref[hh, :, :NOPE] = kn[:, hh * NOPE:(hh + 1) * NOPE]
        k_ref[hh, :, NOPE:] = even if hh % 2 == 0 else odd
        for j in range(nblk):
            vt_ref[hh, j] = vt[hh * VDIM:(hh + 1) * VDIM, j * kvb:(j + 1) * kvb]


def _kv_proj(u_main, u_small, kv_norm_w, w_uk, w_uv_t, cs_tab, sn_tab, b, s):
    t = b * s
    tm = min(s, 512)
    spt = s // tm
    kvb = min(s, KV_BLOCK)
    if tm >= kvb:
        vt_spec = pl.BlockSpec((None, HEADS, tm // kvb, VDIM, kvb), lambda i: (i // spt, 0, i % spt, 0, 0))
    else:
        per = kvb // tm
        vt_spec = pl.BlockSpec((None, HEADS, 1, VDIM, tm),
                               lambda i: (i // spt, 0, (i % spt) // per, 0, (i % spt) % per))
    tab = pl.BlockSpec((tm, LANES), lambda i: (i, 0))
    const = lambda shape: pl.BlockSpec(shape, lambda i: (0, 0))
    return pl.pallas_call(
        _kv_kernel,
        out_shape=(jax.ShapeDtypeStruct((b, HEADS, s, 2 * LANES), BF16),
                   jax.ShapeDtypeStruct((b, HEADS, s // kvb, VDIM, kvb), BF16)),
        grid=(t // tm,),
        in_specs=[pl.BlockSpec((tm, KV_RANK), lambda i: (i, OFF_CKV // KV_RANK)),
                  const((1, KV_RANK)), const((KV_RANK, HEADS * NOPE)), const((HEADS * VDIM, KV_RANK)),
                  tab, tab, tab],
        out_specs=(pl.BlockSpec((None, HEADS, tm, 2 * LANES), lambda i: (i // spt, 0, i % spt, 0)), vt_spec),
        compiler_params=_params("parallel"),
        name="kv_proj",
    )(u_main, kv_norm_w.reshape(1, KV_RANK), w_uk, w_uv_t, u_small, cs_tab, sn_tab)


Q_BLOCK = 256
Q_SUBBLOCKS = 4
KV_UNROLL = 2


def _sublane_all(x, op):
    for shift in (4, 2, 1):
        x = op(x, pltpu.roll(x, shift, 0))
    return x


def _attn_kernel(qn_ref, qr_ref, cs_ref, sn_ref, k_ref, vt_ref, o_ref, qt_sc, s_sc, m_sc, l_sc, acc_sc, *, scale):
    nq = m_sc.shape[0]
    nblk, _, kvb = vt_ref.shape
    kgrp = kvb // SUBLANES
    vgrp = VDIM // SUBLANES
    qr = qr_ref[...].astype(F32)
    roped = qr * cs_ref[...] + _rope_partner(qr) * sn_ref[...]
    q = jnp.concatenate([qn_ref[...].astype(F32), roped], axis=1) * (scale * LOG2_E)
    qt_sc[...] = q.T.astype(BF16)
    m_sc[...] = jnp.full(m_sc.shape, -jnp.inf, F32)
    l_sc[...] = jnp.zeros(l_sc.shape, F32)
    acc_sc[...] = jnp.zeros(acc_sc.shape, F32)

    def scores(c, j):
        kc = k_ref[pl.ds(pl.multiple_of(c * kvb, kvb), kvb), :]
        s_sc[j] = _dot(kc, qt_sc[:, j * Q_BLOCK:(j + 1) * Q_BLOCK])

    for j in range(nq):
        scores(0, j)

    def body(c, carry):
        nxt = jnp.minimum(c + 1, nblk - 1)
        vtc = vt_ref[c]
        for j in range(nq):
            sc = s_sc[j].reshape(kgrp, SUBLANES, Q_BLOCK)
            m_prev = m_sc[j]
            m_new = jnp.maximum(m_prev, _sublane_all(jnp.max(sc, axis=0), jnp.maximum))
            alpha = jnp.exp2(m_prev - m_new)
            p = jnp.exp2(sc - m_new[None])
            l_sc[j] = alpha * l_sc[j] + jnp.sum(p, axis=0)
            pv = _dot(vtc, p.reshape(kvb, Q_BLOCK).astype(BF16))
            scores(nxt, j)
            acc = acc_sc[j].reshape(vgrp, SUBLANES, Q_BLOCK) * alpha[None]
            acc_sc[j] = acc.reshape(VDIM, Q_BLOCK) + pv
            m_sc[j] = m_new
        return carry

    lax.fori_loop(0, nblk, body, 0, unroll=min(KV_UNROLL, nblk))
    for j in range(nq):
        inv = 1.0 / _sublane_all(l_sc[j], jnp.add)
        out_t = (acc_sc[j].reshape(vgrp, SUBLANES, Q_BLOCK) * inv[None]).reshape(VDIM, Q_BLOCK)
        o_ref[j * Q_BLOCK:(j + 1) * Q_BLOCK, :] = out_t.T.astype(o_ref.dtype)


def _attention(u_main, k, vt, cs_tab, sn_tab, b, s):
    t = b * s
    tq = min(s, Q_SUBBLOCKS * Q_BLOCK)
    nq = tq // Q_BLOCK
    qpt = s // tq
    kvb = vt.shape[-1]
    tab = pl.BlockSpec((tq, LANES), lambda bb, h, qi: (bb * qpt + qi, 0))
    kern = functools.partial(_attn_kernel, scale=float(NOPE + ROPE) ** -0.5)
    return pl.pallas_call(
        kern,
        out_shape=jax.ShapeDtypeStruct((t, HEADS * VDIM), BF16),
        grid=(b, HEADS, qpt),
        in_specs=[pl.BlockSpec((tq, NOPE), lambda bb, h, qi: (bb * qpt + qi, h)),
                  pl.BlockSpec((tq, LANES), lambda bb, h, qi: (bb * qpt + qi, OFF_QR // LANES + h // 2)),
                  tab, tab,
                  pl.BlockSpec((None, None, s, 2 * LANES), lambda bb, h, qi: (bb, h, 0, 0)),
                  pl.BlockSpec((None, None, s // kvb, VDIM, kvb), lambda bb, h, qi: (bb, h, 0, 0, 0))],
        out_specs=pl.BlockSpec((tq, VDIM), lambda bb, h, qi: (bb * qpt + qi, h)),
        scratch_shapes=[pltpu.VMEM((2 * LANES, tq), BF16), pltpu.VMEM((nq, kvb, Q_BLOCK), F32),
                        pltpu.VMEM((nq, SUBLANES, Q_BLOCK), F32), pltpu.VMEM((nq, SUBLANES, Q_BLOCK), F32),
                        pltpu.VMEM((nq, VDIM, Q_BLOCK), F32)],
        compiler_params=_params("parallel", "parallel", "arbitrary"),
        name="mla_attention",
    )(u_main, u_main, cs_tab, sn_tab, k, vt)


HALO = BF16_ROWS
CONV_COLS = 512
assert all(o % CONV_COLS == 0 for o in (OFF_XBC, OFF_SCB, OFF_SCC, OFF_SCX))


def _fill_ext(ext_ref, prev, cur, nxt, first, last):
    tm = cur.shape[0]
    ext_ref[:HALO, :] = jnp.where(first, 0.0, prev.astype(F32))
    ext_ref[HALO:HALO + tm, :] = cur.astype(F32)
    ext_ref[HALO + tm:, :] = jnp.where(last, 0.0, nxt.astype(F32))


def _taps(ext_ref, w_ref, ksize, tm):
    pad = ksize // 2
    acc = None
    for kk in range(ksize):
        term = ext_ref[pl.ds(HALO - pad + kk, tm), :] * w_ref[kk:kk + 1, :]
        acc = term if acc is None else acc + term
    return acc


def _ssm_conv_kernel(prev_ref, cur_ref, next_ref, w_ref, b_ref, o_ref, ext_ref, *, tiles_per_seq):
    i = pl.program_id(0)
    tm = cur_ref.shape[0]
    _fill_ext(ext_ref, prev_ref[...], cur_ref[...], next_ref[...],
              i % tiles_per_seq == 0, i % tiles_per_seq == tiles_per_seq - 1)
    o_ref[...] = _silu(_taps(ext_ref, w_ref, SSM_CONV, tm) + b_ref[...]).astype(o_ref.dtype)


def _halo_specs(tm, tc, col0, n_rows):
    r = tm // HALO
    last = n_rows // HALO - 1
    cb = col0 // tc
    return [pl.BlockSpec((HALO, tc), lambda i, j: (jnp.maximum(i * r - 1, 0), cb + j)),
            pl.BlockSpec((tm, tc), lambda i, j: (i, cb + j)),
            pl.BlockSpec((HALO, tc), lambda i, j: (jnp.minimum((i + 1) * r, last), cb + j))]


def _ssm_conv(u_main, conv_w, conv_b, s):
    t = u_main.shape[0]
    tm = min(s, 512)
    tc = CONV_COLS
    kern = functools.partial(_ssm_conv_kernel, tiles_per_seq=s // tm)
    return pl.pallas_call(
        kern,
        out_shape=jax.ShapeDtypeStruct((t, XBC), BF16),
        grid=(t // tm, XBC // tc),
        in_specs=_halo_specs(tm, tc, OFF_XBC, t) + [
            pl.BlockSpec((SSM_CONV, tc), lambda i, j: (0, j)),
            pl.BlockSpec((1, tc), lambda i, j: (0, j))],
        out_specs=pl.BlockSpec((tm, tc), lambda i, j: (i, j)),
        scratch_shapes=[pltpu.VMEM((tm + 2 * HALO, tc), F32)],
        compiler_params=_params("parallel", "parallel"),
        name="ssm_conv",
    )(u_main, u_main, u_main, conv_w, conv_b.reshape(1, XBC))


def _sconv_kernel(cp_ref, cc_ref, cn_ref, xp_ref, xc_ref, xn_ref, b_ref, w_ref, o_ref, ext_ref, *, tiles_per_seq):
    i = pl.program_id(0)
    tm = cc_ref.shape[0]
    f = lambda r: r[...].astype(F32)
    _fill_ext(ext_ref, f(cp_ref) * f(xp_ref), f(cc_ref) * f(xc_ref), f(cn_ref) * f(xn_ref),
              i % tiles_per_seq == 0, i % tiles_per_seq == tiles_per_seq - 1)
    o_ref[...] = (f(b_ref) * _taps(ext_ref, w_ref, SCONV_K, tm)).astype(o_ref.dtype)


def _short_conv(u_main, sconv_w, s):
    t = u_main.shape[0]
    tm = min(s, 512)
    tc = CONV_COLS
    kern = functools.partial(_sconv_kernel, tiles_per_seq=s // tm)
    return pl.pallas_call(
        kern,
        out_shape=jax.ShapeDtypeStruct((t, D), BF16),
        grid=(t // tm, D // tc),
        in_specs=_halo_specs(tm, tc, OFF_SCC, t) + _halo_specs(tm, tc, OFF_SCX, t) + [
            pl.BlockSpec((tm, tc), lambda i, j: (i, OFF_SCB // tc + j)),
            pl.BlockSpec((SCONV_K, tc), lambda i, j: (0, j))],
        out_specs=pl.BlockSpec((tm, tc), lambda i, j: (i, j)),
        scratch_shapes=[pltpu.VMEM((tm + 2 * HALO, tc), F32)],
        compiler_params=_params("parallel", "parallel"),
        name="short_conv",
    )(u_main, u_main, u_main, u_main, u_main, u_main, u_main, sconv_w)


FWD_COL = 2 * SSM_HEADS
BWD_COL = 3 * SSM_HEADS
PAIRS = SSM_HEADS // 2
PAIRS_PER_GROUP = PAIRS // SSM_GROUPS


def _cumsum(x, axis, reverse):
    n = x.shape[axis]
    idx = lax.broadcasted_iota(I32, x.shape, axis)
    k = 1
    while k < n:
        if reverse:
            x = x + jnp.where(idx < n - k, pltpu.roll(x, n - k, axis), 0.0)
        else:
            x = x + jnp.where(idx >= k, pltpu.roll(x, k, axis), 0.0)
        k *= 2
    return x


def _spread_heads(vals, sel, base, exact):
    lane = _lane_iota(vals.shape)
    v = jnp.where(lane >= base, jnp.where(lane < base + SSM_HEADS, vals, 0.0), 0.0)
    hi = v.astype(BF16)
    out = _dot(hi, sel)
    if exact:
        rest = v - hi.astype(F32)
        mid = rest.astype(BF16)
        out = out + _dot(mid, sel) + _dot((rest - mid.astype(F32)).astype(BF16), sel)
    return out


def _ssd_direction(xbc_ref, us_ref, ust_ref, sel, dtb_row, a_row, dtb_col, a_col, d_row, y_ref, st_ref, *, reverse):
    base = BWD_COL if reverse else FWD_COL
    edge = 0 if reverse else CHUNK - 1
    dt_c = _softplus(us_ref[...] + dtb_row)
    cum_c = _cumsum(dt_c * a_row, 0, reverse)
    dt_r = _softplus(ust_ref[...] + dtb_col)
    cum_r = _cumsum(dt_r * a_col, 1, reverse)
    tot = cum_c[edge:edge + 1, :]
    dt_x = _spread_heads(dt_c, sel, base, False)
    ecum_x = _spread_heads(jnp.exp(cum_c), sel, base, False)
    dec_x = _spread_heads(jnp.exp(tot - cum_c) * dt_c, sel, base, False)
    etot_x = _spread_heads(jnp.broadcast_to(jnp.exp(tot), (SUBLANES, LANES)), sel, base, True)[0:1, :]

    li = lax.broadcasted_iota(I32, (CHUNK, CHUNK), 0)
    si = lax.broadcasted_iota(I32, (CHUNK, CHUNK), 1)
    mask = (li <= si) if reverse else (li >= si)
    lane = _lane_iota((CHUNK, LANES))
    lo = lane < SSM_P

    for g in range(SSM_GROUPS):
        bg = xbc_ref[:, SSM_INNER + g * SSM_N:SSM_INNER + (g + 1) * SSM_N]
        cg = xbc_ref[:, SSM_INNER + (SSM_GROUPS + g) * SSM_N:SSM_INNER + (SSM_GROUPS + g + 1) * SSM_N]
        cb = _dot_nt(cg, bg)
        for j in range(PAIRS_PER_GROUP):
            pr = g * PAIRS_PER_GROUP + j
            c1, c2 = base + 2 * pr, base + 2 * pr + 1
            cols = slice(pr * LANES, (pr + 1) * LANES)

            xs = xbc_ref[:, cols].astype(F32)
            xdt = xs * dt_x[:, cols]
            scores = []
            for c in (c1, c2):
                seg = cum_c[:, c:c + 1] - cum_r[c:c + 1, :]
                scores.append((cb * jnp.exp(jnp.where(mask, seg, -jnp.inf))).astype(BF16))
            xb = xdt.astype(BF16)
            zero = jnp.zeros_like(xb)
            x_bd = jnp.concatenate([jnp.where(lo, xb, zero), jnp.where(lo, zero, xb)], axis=0)
            y = _dot(jnp.concatenate(scores, axis=1), x_bd)
            st = st_ref[g, :, j * LANES:(j + 1) * LANES]
            y = y + _dot(cg, st.astype(BF16)) * ecum_x[:, cols]
            if not reverse:
                y = y + xs * d_row[:, cols]
            y_ref[:, cols] = y.astype(y_ref.dtype)
            new = _dot_tn(bg, (xs * dec_x[:, cols]).astype(BF16))
            st_ref[g, :, j * LANES:(j + 1) * LANES] = st * etot_x[:, cols] + new


def _ssd_kernel(xf_ref, usf_ref, ustf_ref, xb_ref, usb_ref, ustb_ref, sel_ref, dtb_row_ref, alog_row_ref,
                dtb_col_ref, alog_col_ref, d_ref, yf_ref, yb_ref, stf_ref, stb_ref):
    @pl.when(pl.program_id(1) == 0)
    def _():
        stf_ref[...] = jnp.zeros(stf_ref.shape, F32)
        stb_ref[...] = jnp.zeros(stb_ref.shape, F32)

    dtb_row, dtb_col = dtb_row_ref[...], dtb_col_ref[...]
    a_row, a_col = -jnp.exp(alog_row_ref[...]), -jnp.exp(alog_col_ref[...])
    d_row = d_ref[...]
    sel = sel_ref[...]
    _ssd_direction(xf_ref, usf_ref, ustf_ref, sel, dtb_row, a_row, dtb_col, a_col, d_row, yf_ref, stf_ref,
                   reverse=False)
    _ssd_direction(xb_ref, usb_ref, ustb_ref, sel, dtb_row, a_row, dtb_col, a_col, d_row, yb_ref, stb_ref,
                   reverse=True)


def _ssd_scan(xbc_conv, u_small, u_small_t, a_log, dt_bias, d_skip, b, s):
    t = b * s
    nc = s // CHUNK
    pad = jnp.zeros((2 * SSM_HEADS,), F32)
    dtb = jnp.concatenate([pad, dt_bias.reshape(-1)])
    alog = jnp.concatenate([pad, a_log.reshape(-1)])
    d_row = jnp.repeat(d_skip, SSM_P)[None, :]
    head_of_lane = np.concatenate([np.full(2 * SSM_HEADS, -1), np.arange(SSM_HEADS), np.arange(SSM_HEADS)])
    sel = jnp.asarray(head_of_lane[:, None] == np.repeat(np.arange(SSM_HEADS), SSM_P)[None, :], BF16)
    fw = lambda bb, i: bb * nc + i
    bw = lambda bb, i: bb * nc + nc - 1 - i
    row = lambda w: pl.BlockSpec((1, w), lambda bb, i: (0, 0))
    col = pl.BlockSpec((LANES, 1), lambda bb, i: (0, 0))

    def chunk_specs(idx):
        return [pl.BlockSpec((CHUNK, XBC), lambda bb, i: (idx(bb, i), 0)),
                pl.BlockSpec((CHUNK, LANES), lambda bb, i: (idx(bb, i), 0)),
                pl.BlockSpec((LANES, CHUNK), lambda bb, i: (0, idx(bb, i)))]

    return pl.pallas_call(
        _ssd_kernel,
        out_shape=(jax.ShapeDtypeStruct((t, SSM_INNER), BF16),) * 2,
        grid=(b, nc),
        in_specs=chunk_specs(fw) + chunk_specs(bw) + [
            pl.BlockSpec((LANES, SSM_INNER), lambda bb, i: (0, 0)),
            row(LANES), row(LANES), col, col, row(SSM_INNER)],
        out_specs=(pl.BlockSpec((CHUNK, SSM_INNER), lambda bb, i: (fw(bb, i), 0)),
                   pl.BlockSpec((CHUNK, SSM_INNER), lambda bb, i: (bw(bb, i), 0))),
        scratch_shapes=[pltpu.VMEM((SSM_GROUPS, SSM_N, SSM_INNER // SSM_GROUPS), F32)] * 2,
        compiler_params=_params("parallel", "arbitrary"),
        name="ssd_scan",
    )(xbc_conv, u_small, u_small_t, xbc_conv, u_small, u_small_t, sel,
      dtb[None, :], alog[None, :], dtb[:, None], alog[:, None], d_row)


def _ssm_post_kernel(yf_ref, yb_ref, z_ref, w_ref, o_ref):
    y = (yf_ref[...].astype(F32) + yb_ref[...].astype(F32)) * _silu(z_ref[...].astype(F32))
    o_ref[...] = _rms(y, w_ref[...]).astype(o_ref.dtype)


def _ssm_post(y_fwd, y_bwd, u_main, norm_w):
    t = y_fwd.shape[0]
    tm = min(t, 512)
    gw = SSM_INNER // SSM_GROUPS
    blk = pl.BlockSpec((tm, gw), lambda i, g: (i, g))
    return pl.pallas_call(
        _ssm_post_kernel,
        out_shape=jax.ShapeDtypeStruct((t, SSM_INNER), BF16),
        grid=(t // tm, SSM_GROUPS),
        in_specs=[blk, blk, pl.BlockSpec((tm, gw), lambda i, g: (i, OFF_Z // gw + g)),
                  pl.BlockSpec((1, gw), lambda i, g: (0, g))],
        out_specs=blk,
        compiler_params=_params("parallel", "parallel"),
        name="ssm_gate_norm",
    )(y_fwd, y_bwd, u_main, norm_w.reshape(1, SSM_INNER))


def _merge_kernel(ya_ref, ys_ref, yc_ref, pa_ref, ps_ref, pc_ref, ga_ref, gs_ref, gc_ref, o_ref):
    g = lambda r: _sigmoid(r[...].astype(F32))
    m = (g(ga_ref) * _dot(ya_ref[...], pa_ref[...]) + g(gs_ref) * _dot(ys_ref[...], ps_ref[...])
         + g(gc_ref) * _dot(yc_ref[...], pc_ref[...]))
    o_ref[...] = m.astype(o_ref.dtype)


def _merge(y_att, y_ssm, y_conv, w_branch, u_main):
    t = y_att.shape[0]
    tm = min(t, 512)
    tn = 512
    yb = pl.BlockSpec((tm, D), lambda i, j: (i, 0))
    pspec = lambda r: pl.BlockSpec((D, tn), lambda i, j: (r, j))
    gspec = lambda r: pl.BlockSpec((tm, tn), lambda i, j: (i, OFF_GATE // tn + r * (D // tn) + j))
    return pl.pallas_call(
        _merge_kernel,
        out_shape=jax.ShapeDtypeStruct((t, D), BF16),
        grid=(t // tm, D // tn),
        in_specs=[yb, yb, yb, pspec(0), pspec(1), pspec(2), gspec(0), gspec(1), gspec(2)],
        out_specs=pl.BlockSpec((tm, tn), lambda i, j: (i, j)),
        compiler_params=_params("parallel", "arbitrary"),
        name="branch_merge",
    )(y_att, y_ssm, y_conv, w_branch, w_branch, w_branch, u_main, u_main, u_main)


def _split_bf16(x):
    hi = x.astype(BF16)
    lo = (x - hi.astype(F32)).astype(BF16)
    return hi, lo


def _wo_kernel(m_ref, h_ref, wo_ref, nw_ref, rhi_ref, rlo_ref, rb_ref, h1_ref, t_ref, lg_ref):
    h1 = h_ref[...] + _dot(m_ref[...], wo_ref[...])
    h1_ref[...] = h1
    tn = _rms(h1, nw_ref[...])
    for sl in range(SLAB):
        t_ref[pl.ds(sl, tn.shape[0], stride=SLAB), :] = tn[:, sl * LANES:(sl + 1) * LANES]
    hi, lo = _split_bf16(tn)
    lg_ref[...] = _dot(hi, rhi_ref[...]) + _dot(lo, rhi_ref[...]) + _dot(hi, rlo_ref[...]) + rb_ref[...]


def _out_proj(merged, h, w_o, ffn_norm_w, r_hi, r_lo, r_bias):
    t = h.shape[0]
    tm = min(t, 256)
    blk = pl.BlockSpec((tm, D), lambda i: (i, 0))
    const = lambda shape: pl.BlockSpec(shape, lambda i: (0, 0))
    lg = pl.BlockSpec((tm, LANES), lambda i: (i, 0))
    return pl.pallas_call(
        _wo_kernel,
        out_shape=(jax.ShapeDtypeStruct((t, D), F32), jax.ShapeDtypeStruct((t * SLAB, LANES), F32),
                   jax.ShapeDtypeStruct((t, LANES), F32)),
        grid=(t // tm,),
        in_specs=[blk, blk, const((D, D)), const((1, D)), const((D, LANES)), const((D, LANES)),
                  const((1, LANES))],
        out_specs=(blk, pl.BlockSpec((tm * SLAB, LANES), lambda i: (i, 0)), lg),
        compiler_params=_params("parallel"),
        name="out_proj_router",
    )(merged, h, w_o, ffn_norm_w.reshape(1, D), r_hi, r_lo, r_bias)


def _route_kernel(lg_ref, id_ref, wt_ref, rk_ref, cnt_ref, carry_sc):
    @pl.when(pl.program_id(0) == 0)
    def _():
        carry_sc[...] = jnp.zeros(carry_sc.shape, F32)

    lg = lg_ref[...]
    lane = _lane_iota(lg.shape).astype(F32)
    big = float(LANES)
    ninf = -jnp.inf
    first = lambda hit: jnp.min(jnp.where(hit, lane, big), axis=1, keepdims=True)

    gl = jnp.where(lane < N_GROUPS, lg, ninf)
    gmax = jnp.max(gl, axis=1, keepdims=True)
    gidx = first(gl == gmax)
    g_w = 1.0 / jnp.sum(jnp.exp(gl - gmax), axis=1, keepdims=True)

    in_group = (lane >= N_GROUPS + gidx * EPG) & (lane < N_GROUPS + (gidx + 1.0) * EPG)
    el = jnp.where(in_group, lg, ninf)
    v0 = jnp.max(el, axis=1, keepdims=True)
    i0 = first(el == v0)
    el = jnp.where(lane == i0, ninf, el)
    v1 = jnp.max(el, axis=1, keepdims=True)
    i1 = first(el == v1)
    e1 = jnp.exp(v1 - v0)
    w0 = g_w / (1.0 + e1)
    w1 = g_w * e1 / (1.0 + e1)
    id_ref[...] = jnp.where(lane == 0.0, i0, i1).astype(I32) - N_GROUPS
    wt_ref[...] = jnp.where(lane == 0.0, w0, w1)

    hit0 = lane == i0 - N_GROUPS
    hit1 = lane == i1 - N_GROUPS
    onehot = jnp.where(hit0, 1.0, jnp.where(hit1, 1.0, 0.0))
    tm = lg.shape[0]
    earlier = lax.broadcasted_iota(I32, (tm, tm), 1) < lax.broadcasted_iota(I32, (tm, tm), 0)
    carry = carry_sc[0:1, :]
    before = _dot(jnp.where(earlier, 1.0, 0.0).astype(BF16), onehot.astype(BF16)) + carry
    r0 = jnp.sum(jnp.where(hit0, before, 0.0), axis=1, keepdims=True)
    r1 = jnp.sum(jnp.where(hit1, before, 0.0), axis=1, keepdims=True)
    rk_ref[...] = jnp.where(lane == 0.0, r0, r1).astype(I32)
    carry_sc[...] = jnp.broadcast_to(carry + jnp.sum(onehot, axis=0, keepdims=True), carry_sc.shape)
    cnt_ref[...] = carry_sc[...]


def _route(logits):
    t = logits.shape[0]
    tm = min(t, 512)
    blk = pl.BlockSpec((tm, LANES), lambda i: (i, 0))
    cnt = pl.BlockSpec((SUBLANES, LANES), lambda i: (0, 0))
    return pl.pallas_call(
        _route_kernel,
        out_shape=(jax.ShapeDtypeStruct((t, LANES), I32), jax.ShapeDtypeStruct((t, LANES), F32),
                   jax.ShapeDtypeStruct((t, LANES), I32), jax.ShapeDtypeStruct((SUBLANES, LANES), F32)),
        grid=(t // tm,),
        in_specs=[blk],
        out_specs=(blk, blk, blk, cnt),
        scratch_shapes=[pltpu.VMEM((SUBLANES, LANES), F32)],
        compiler_params=_params("arbitrary"),
        name="route",
    )(logits)


def _dispatch_plan(ids, wts, ranks, counts, t, tile):
    n_rows = 2 * t
    n_tiles = n_rows // tile + N_EXPERTS
    n_pos = n_tiles * tile
    e = ids[:, :2].reshape(-1)
    w = wts[:, :2].reshape(-1)
    rank = ranks[:, :2].reshape(-1)
    counts = counts[0, :N_EXPERTS].astype(I32)
    padded = ((counts + tile - 1) // tile) * tile
    ends = jnp.cumsum(padded)
    starts = ends - padded
    pos = starts[e] + rank
    r = jnp.arange(n_rows, dtype=I32)
    packed = jnp.stack([r, lax.bitcast_convert_type(w, I32)], axis=1)
    slots = jnp.full((n_pos, 2), -1, I32).at[pos].set(packed)
    src = slots[:, 0]
    valid = src >= 0
    tile_start = jnp.arange(n_tiles, dtype=I32) * tile
    tile_e = jnp.minimum(jnp.searchsorted(ends, tile_start, side="right"), N_EXPERTS - 1).astype(I32)
    tile_rows = jnp.clip((starts + counts)[tile_e] - tile_start, 0, tile).astype(I32)
    row_tok = jnp.where(valid, src // 2, 0)
    row_dst = jnp.where(valid, (src % 2) * t + src // 2, 0)
    row_w = jnp.where(valid, lax.bitcast_convert_type(slots[:, 1], F32), 0.0)
    n_used = (ends[-1] // tile).astype(I32).reshape(1)
    return (row_tok.reshape(n_tiles, 1, tile), row_dst.reshape(n_tiles, 1, tile),
            row_w.reshape(n_pos, 1), tile_e, tile_rows, n_used)


DMA_UNROLL = 8


def _moe_kernel(te_ref, tr_ref, nu_ref, tok_ref, tokn_ref, dst_ref, rw_ref, t_hbm, wg_ref, wu_ref, wd_ref, o_hbm,
                xbuf, ybuf, wg_sc, wu_sc, wd_sc, gsem, ssem):
    i = pl.program_id(0)
    n_used = nu_ref[0]
    tile = rw_ref.shape[0]
    rows = tile * PITCH
    slot = i % 2

    def start_gather(tok, slot_):
        base = slot_ * rows

        def body(r, c):
            dst = xbuf.at[pl.ds(pl.multiple_of(base + r * PITCH, SUBLANES), SLAB), :]
            pltpu.make_async_copy(t_hbm.at[tok[0, 0, r]], dst, gsem.at[slot_]).start()
            return c

        lax.fori_loop(0, tile, body, 0, unroll=DMA_UNROLL)

    def wait_rows(buf, n_rows, sem):
        done = buf.at[pl.ds(0, n_rows * SLAB), :]
        pltpu.make_async_copy(done, done, sem).wait()

    @pl.when(i == 0)
    def _():
        start_gather(tok_ref, 0)

    @pl.when(i < n_used)
    def _():
        @pl.when((i == 0) | (te_ref[i] != te_ref[jnp.maximum(i - 1, 0)]))
        def _():
            wg_sc[...] = wg_ref[...].astype(BF16)
            wu_sc[...] = wu_ref[...].astype(BF16)
            wd_sc[...] = wd_ref[...].astype(BF16)

        @pl.when(i + 1 < n_used)
        def _():
            start_gather(tokn_ref, 1 - slot)

        wait_rows(xbuf, tile, gsem.at[slot])
        base = slot * rows
        x = jnp.concatenate([xbuf[pl.ds(base + sl, tile, stride=PITCH), :] for sl in range(SLAB)],
                            axis=1).astype(BF16)
        gate = _dot(x, wg_sc[...])
        mid = (_silu(gate) * _dot(x, wu_sc[...])).astype(BF16)
        y = _dot(mid, wd_sc[...]) * rw_ref[...]

        @pl.when(i > 0)
        def _():
            wait_rows(ybuf, tr_ref[jnp.maximum(i - 1, 0)], ssem)

        for sl in range(SLAB):
            ybuf[pl.ds(sl, tile, stride=PITCH), :] = y[:, sl * LANES:(sl + 1) * LANES]

        n_real = tr_ref[i]

        def s_body(r, c):
            @pl.when(r < n_real)
            def _():
                src = ybuf.at[pl.ds(pl.multiple_of(r * PITCH, SUBLANES), SLAB), :]
                pltpu.make_async_copy(src, o_hbm.at[dst_ref[0, 0, r]], ssem).start()
            return c

        lax.fori_loop(0, tile, s_body, 0, unroll=DMA_UNROLL)

        @pl.when(i == n_used - 1)
        def _():
            wait_rows(ybuf, n_real, ssem)


def _moe(t_norm, plan, w_gate, w_up, w_down):
    t = t_norm.shape[0] // SLAB
    row_tok, row_dst, row_w, tile_e, tile_rows, n_used = plan
    n_tiles, _, tile = row_tok.shape
    t_slab = t_norm.reshape(t, SLAB, LANES)
    smem = pl.BlockSpec((1, 1, tile), lambda i, te, tr, nu: (i, 0, 0), memory_space=pltpu.SMEM)
    smem_next = pl.BlockSpec((1, 1, tile), lambda i, te, tr, nu: (jnp.minimum(i + 1, n_tiles - 1), 0, 0),
                             memory_space=pltpu.SMEM)
    wspec = lambda a, b: pl.BlockSpec((None, a, b), lambda i, te, tr, nu: (te[i], 0, 0))
    out = pl.pallas_call(
        _moe_kernel,
        out_shape=jax.ShapeDtypeStruct((2 * t, SLAB, LANES), F32),
        grid_spec=pltpu.PrefetchScalarGridSpec(
            num_scalar_prefetch=3,
            grid=(n_tiles,),
            in_specs=[smem, smem_next, smem,
                      pl.BlockSpec((tile, 1), lambda i, te, tr, nu: (i, 0)),
                      pl.BlockSpec(memory_space=pl.ANY),
                      wspec(D, D_EXPERT), wspec(D, D_EXPERT), wspec(D_EXPERT, D)],
            out_specs=pl.BlockSpec(memory_space=pl.ANY),
            scratch_shapes=[pltpu.VMEM((2 * tile * PITCH, LANES), F32), pltpu.VMEM((tile * PITCH, LANES), F32),
                            pltpu.VMEM((D, D_EXPERT), BF16), pltpu.VMEM((D, D_EXPERT), BF16),
                            pltpu.VMEM((D_EXPERT, D), BF16),
                            pltpu.SemaphoreType.DMA((2,)), pltpu.SemaphoreType.DMA(())]),
        compiler_params=_params("arbitrary"),
        name="moe_experts",
    )(tile_e, tile_rows, n_used, row_tok, row_tok, row_dst, row_w, t_slab, w_gate, w_up, w_down)
    return out.reshape(2 * t * SLAB, LANES)


def kernel(x, positions, attn_norm_w, w_in, kv_norm_w, w_uk, w_uv, ssm_conv_w, ssm_conv_b, ssm_A_log, ssm_dt_bias, ssm_D, ssm_norm_w, sconv_w, w_branch, w_o, ffn_norm_w, router_group_w, router_group_b, router_expert_w, router_expert_b, expert_w_gate, expert_w_up, expert_w_down, final_norm_w):
    b, s, _ = x.shape
    t = b * s
    depth = w_in.shape[0]
    h = x.reshape(t, D)
    cs_tab, sn_tab = _rope_tables(positions)
    moe_out = None
    for l in range(depth):
        if moe_out is None:
            n = _norm(h, attn_norm_w[l], BF16)
        else:
            h, n = _add_norm(h, moe_out, attn_norm_w[l], BF16)
        w_perm = _permute_in_proj(w_in[l])
        w_small = w_perm[:, OFF_SMALL:OFF_SMALL + LANES]
        u_main, u_small, u_small_t = _in_proj(n, w_perm, w_small, w_small.T)

        k, vt = _kv_proj(u_main, u_small, kv_norm_w[l], w_uk[l].astype(BF16), w_uv[l].T.astype(BF16),
                         cs_tab, sn_tab, b, s)
        y_att = _attention(u_main, k, vt, cs_tab, sn_tab, b, s)

        xbc_conv = _ssm_conv(u_main, ssm_conv_w[l], ssm_conv_b[l], s)
        y_fwd, y_bwd = _ssd_scan(xbc_conv, u_small, u_small_t, ssm_A_log[l], ssm_dt_bias[l], ssm_D[l], b, s)
        y_ssm = _ssm_post(y_fwd, y_bwd, u_main, ssm_norm_w[l])

        y_conv = _short_conv(u_main, sconv_w[l], s)

        merged = _merge(y_att, y_ssm, y_conv, w_branch[l].astype(BF16), u_main)

        r_w = jnp.concatenate([router_group_w[l], router_expert_w[l],
                               jnp.zeros((D, LANES - N_GROUPS - N_EXPERTS), F32)], axis=1)
        r_b = jnp.concatenate([router_group_b[l], router_expert_b[l],
                               jnp.zeros((LANES - N_GROUPS - N_EXPERTS,), F32)])[None, :]
        r_hi, r_lo = _split_bf16(r_w)
        h, t_norm, logits = _out_proj(merged, h, w_o[l].astype(BF16), ffn_norm_w[l], r_hi, r_lo, r_b)

        ids, wts, ranks, counts = _route(logits)
        plan = _dispatch_plan(ids, wts, ranks, counts, t, MOE_TILE)
        moe_out = _moe(t_norm, plan, expert_w_gate[l], expert_w_up[l], expert_w_down[l])

    h, out = _add_norm(h, moe_out, final_norm_w, F32)
    return out.reshape(b, s, D)
```

```python
import functools

import numpy as np
import jax
import jax.numpy as jnp
from jax import lax
from jax.experimental import pallas as pl
from jax.experimental.pallas import tpu as pltpu

F32 = jnp.float32
BF16 = jnp.bfloat16
I32 = jnp.int32

D = 2048
EPS = 1e-6
LOG2_E = 1.4426950408889634
HEADS = 16
NOPE = 128
ROPE = 64
VDIM = 128
KV_RANK = 512
ROPE_THETA = 10000.0
SSM_HEADS = 32
SSM_P = 64
SSM_INNER = SSM_HEADS * SSM_P
SSM_GROUPS = 4
SSM_N = 128
SSM_CONV = 5
CHUNK = 128
XBC = SSM_INNER + 2 * SSM_GROUPS * SSM_N
SCONV_K = 3
N_GROUPS = 8
EPG = 4
N_EXPERTS = N_GROUPS * EPG
D_EXPERT = 512
Q_DIM = HEADS * (NOPE + ROPE)
IN_COLS = Q_DIM + KV_RANK + ROPE + SSM_INNER + XBC + 2 * SSM_HEADS + 3 * D + 3 * D

LANES = 128
SUBLANES = 8
BF16_ROWS = 16
V7X_VMEM_BYTES = 64 * 1024 * 1024
VMEM_LIMIT = 56 * 1024 * 1024

OFF_QN = 0
OFF_QR = OFF_QN + HEADS * NOPE
OFF_CKV = OFF_QR + HEADS * ROPE
OFF_Z = OFF_CKV + KV_RANK
OFF_XBC = OFF_Z + SSM_INNER
OFF_SCB = OFF_XBC + XBC
OFF_SCC = OFF_SCB + D
OFF_SCX = OFF_SCC + D
OFF_GATE = OFF_SCX + D
OFF_SMALL = OFF_GATE + 3 * D
assert OFF_SMALL + LANES == IN_COLS
IN_PROJ_TN = 1024
U_COLS = -(-IN_COLS // IN_PROJ_TN) * IN_PROJ_TN

SLAB = D // LANES
PITCH = 24
MOE_TILE = 256


def _permute_in_proj(w):
    o_ckv = Q_DIM
    o_kr = o_ckv + KV_RANK
    o_z = o_kr + ROPE
    o_dt = o_z + SSM_INNER + XBC
    o_rest = o_dt + 2 * SSM_HEADS
    q = w[:, :Q_DIM].reshape(D, HEADS, NOPE + ROPE)
    parts = [q[:, :, :NOPE].reshape(D, HEADS * NOPE), q[:, :, NOPE:].reshape(D, HEADS * ROPE),
             w[:, o_ckv:o_kr], w[:, o_z:o_dt], w[:, o_rest:], w[:, o_kr:o_z], w[:, o_dt:o_rest]]
    parts.append(jnp.zeros((D, U_COLS - IN_COLS), w.dtype))
    out = jnp.concatenate([p.astype(BF16) for p in parts], axis=1)
    assert out.shape == (D, U_COLS)
    return out


def _params(*sem):
    return pltpu.CompilerParams(dimension_semantics=tuple(sem), vmem_limit_bytes=VMEM_LIMIT)


def _dot(a, b):
    return jnp.dot(a, b, preferred_element_type=F32)


def _dot_nt(a, b):
    return lax.dot_general(a, b, (((1,), (1,)), ((), ())), preferred_element_type=F32)


def _dot_tn(a, b):
    return lax.dot_general(a, b, (((0,), (0,)), ((), ())), preferred_element_type=F32)


def _sigmoid(x):
    return 1.0 / (1.0 + jnp.exp(-x))


def _silu(x):
    return x * _sigmoid(x)


def _softplus(x):
    return jnp.maximum(x, 0.0) + jnp.log(1.0 + jnp.exp(-jnp.abs(x)))


def _lane_iota(shape):
    return lax.broadcasted_iota(I32, shape, len(shape) - 1)


def _rope_partner(v):
    lane = _lane_iota(v.shape)
    return jnp.where((lane & 32) == 0, pltpu.roll(v, LANES - 32, 1), pltpu.roll(v, 32, 1))


def _rope_tab_kernel(pos_ref, freq_ref, sign_ref, cs_ref, sn_ref):
    ang = pos_ref[...].astype(F32) * freq_ref[...]
    cs_ref[...] = jnp.cos(ang)
    sn_ref[...] = jnp.sin(ang) * sign_ref[...]


def _rope_tables(positions):
    t = positions.size
    tm = min(t, 1024)
    freqs = ROPE_THETA ** (-jnp.arange(0, ROPE, 2, dtype=F32) / ROPE)
    freq_row = jnp.tile(freqs, 4)[None, :]
    sign_row = jnp.tile(jnp.concatenate([-jnp.ones(32, F32), jnp.ones(32, F32)]), 2)[None, :]
    row = pl.BlockSpec((1, LANES), lambda i: (0, 0))
    tab = pl.BlockSpec((tm, LANES), lambda i: (i, 0))
    return pl.pallas_call(
        _rope_tab_kernel,
        out_shape=(jax.ShapeDtypeStruct((t, LANES), F32),) * 2,
        grid=(t // tm,),
        in_specs=[pl.BlockSpec((tm, 1), lambda i: (i, 0)), row, row],
        out_specs=(tab, tab),
        compiler_params=_params("parallel"),
        name="rope_tables",
    )(positions.reshape(t, 1), freq_row, sign_row)


def _rms(x, w):
    return x * lax.rsqrt(jnp.mean(x * x, axis=-1, keepdims=True) + EPS) * w


def _norm_kernel(h_ref, w_ref, n_ref):
    n_ref[...] = _rms(h_ref[...], w_ref[...]).astype(n_ref.dtype)


def _rows_from_slabs(ref, tm):
    return jnp.concatenate([ref[pl.ds(sl, tm, stride=SLAB), :] for sl in range(SLAB)], axis=1)


def _add_norm_kernel(h_ref, o0_ref, o1_ref, w_ref, hs_ref, n_ref):
    tm = h_ref.shape[0]
    hs = h_ref[...] + _rows_from_slabs(o0_ref, tm) + _rows_from_slabs(o1_ref, tm)
    hs_ref[...] = hs
    n_ref[...] = _rms(hs, w_ref[...]).astype(n_ref.dtype)


def _norm(h, w, out_dtype):
    t = h.shape[0]
    tm = min(t, 256)
    blk = pl.BlockSpec((tm, D), lambda i: (i, 0))
    return pl.pallas_call(
        _norm_kernel,
        out_shape=jax.ShapeDtypeStruct((t, D), out_dtype),
        grid=(t // tm,),
        in_specs=[blk, pl.BlockSpec((1, D), lambda i: (0, 0))],
        out_specs=blk,
        compiler_params=_params("parallel"),
        name="rmsnorm",
    )(h, w.reshape(1, D))


def _add_norm(h, o, w, out_dtype):
    t = h.shape[0]
    tm = min(t, 256)
    nt = t // tm
    blk = pl.BlockSpec((tm, D), lambda i: (i, 0))
    return pl.pallas_call(
        _add_norm_kernel,
        out_shape=(jax.ShapeDtypeStruct((t, D), F32), jax.ShapeDtypeStruct((t, D), out_dtype)),
        grid=(nt,),
        in_specs=[blk, pl.BlockSpec((tm * SLAB, LANES), lambda i: (i, 0)),
                  pl.BlockSpec((tm * SLAB, LANES), lambda i: (nt + i, 0)),
                  pl.BlockSpec((1, D), lambda i: (0, 0))],
        out_specs=(blk, blk),
        compiler_params=_params("parallel"),
        name="add_rmsnorm",
    )(h, o, o, w.reshape(1, D))


def _inproj_kernel(n_ref, wm_ref, ws_ref, wst_ref, um_ref, us_ref, ust_ref):
    x = n_ref[...]
    um_ref[...] = _dot(x, wm_ref[...]).astype(um_ref.dtype)

    @pl.when(pl.program_id(1) == 0)
    def _():
        us_ref[...] = _dot(x, ws_ref[...])
        ust_ref[...] = _dot_nt(wst_ref[...], x)


def _in_proj(n, w_main, w_small, w_small_t):
    t = n.shape[0]
    tm = min(t, 1024)
    tn = IN_PROJ_TN
    return pl.pallas_call(
        _inproj_kernel,
        out_shape=(jax.ShapeDtypeStruct((t, U_COLS), BF16),
                   jax.ShapeDtypeStruct((t, LANES), F32),
                   jax.ShapeDtypeStruct((LANES, t), F32)),
        grid=(t // tm, U_COLS // tn),
        in_specs=[pl.BlockSpec((tm, D), lambda i, j: (i, 0)),
                  pl.BlockSpec((D, tn), lambda i, j: (0, j)),
                  pl.BlockSpec((D, LANES), lambda i, j: (0, 0)),
                  pl.BlockSpec((LANES, D), lambda i, j: (0, 0))],
        out_specs=(pl.BlockSpec((tm, tn), lambda i, j: (i, j)),
                   pl.BlockSpec((tm, LANES), lambda i, j: (i, 0)),
                   pl.BlockSpec((LANES, tm), lambda i, j: (0, i))),
        compiler_params=_params("parallel", "arbitrary"),
        name="in_proj",
    )(n, w_main, w_small, w_small_t)


KV_BLOCK = 1024


def _kv_kernel(ckv_ref, nw_ref, wuk_ref, wuvt_ref, us_ref, cs_ref, sn_ref, k_ref, vt_ref):
    nblk, kvb = vt_ref.shape[1], vt_ref.shape[3]
    c = _rms(ckv_ref[...].astype(F32), nw_ref[...]).astype(BF16)
    kn = _dot(c, wuk_ref[...]).astype(k_ref.dtype)
    vt = _dot_nt(wuvt_ref[...], c).astype(vt_ref.dtype)
    u = us_ref[...]
    roped = u * cs_ref[...] + _rope_partner(u) * sn_ref[...]
    lane = _lane_iota(roped.shape)
    even = jnp.where(lane < ROPE, roped, 0.0).astype(k_ref.dtype)
    odd = jnp.where(lane >= ROPE, pltpu.roll(roped, ROPE, 1), 0.0).astype(k_ref.dtype)
    for hh in range(HEADS):
        k_ref[hh, :, :NOPE] = kn[:, hh * NOPE:(hh + 1) * NOPE]
        k_ref[hh, :, NOPE:] = even if hh % 2 == 0 else odd
        for j in range(nblk):
            vt_ref[hh, j] = vt[hh * VDIM:(hh + 1) * VDIM, j * kvb:(j + 1) * kvb]


def _kv_proj(u_main, u_small, kv_norm_w, w_uk, w_uv_t, cs_tab, sn_tab, b, s):
    t = b * s
    tm = min(s, 512)
    spt = s // tm
    kvb = min(s, KV_BLOCK)
    if tm >= kvb:
        vt_spec = pl.BlockSpec((None, HEADS, tm // kvb, VDIM, kvb), lambda i: (i // spt, 0, i % spt, 0, 0))
    else:
        per = kvb // tm
        vt_spec = pl.BlockSpec((None, HEADS, 1, VDIM, tm),
                               lambda i: (i // spt, 0, (i % spt) // per, 0, (i % spt) % per))
    tab = pl.BlockSpec((tm, LANES), lambda i: (i, 0))
    const = lambda shape: pl.BlockSpec(shape, lambda i: (0, 0))
    return pl.pallas_call(
        _kv_kernel,
        out_shape=(jax.ShapeDtypeStruct((b, HEADS, s, 2 * LANES), BF16),
                   jax.ShapeDtypeStruct((b, HEADS, s // kvb, VDIM, kvb), BF16)),
        grid=(t // tm,),
        in_specs=[pl.BlockSpec((tm, KV_RANK), lambda i: (i, OFF_CKV // KV_RANK)),
                  const((1, KV_RANK)), const((KV_RANK, HEADS * NOPE)), const((HEADS * VDIM, KV_RANK)),
                  tab, tab, tab],
        out_specs=(pl.BlockSpec((None, HEADS, tm, 2 * LANES), lambda i: (i // spt, 0, i % spt, 0)), vt_spec),
        compiler_params=_params("parallel"),
        name="kv_proj",
    )(u_main, kv_norm_w.reshape(1, KV_RANK), w_uk, w_uv_t, u_small, cs_tab, sn_tab)


Q_BLOCK = 256
Q_SUBBLOCKS = 4
KV_UNROLL = 2


def _sublane_all(x, op):
    for shift in (4, 2, 1):
        x = op(x, pltpu.roll(x, shift, 0))
    return x


def _attn_kernel(qn_ref, qr_ref, cs_ref, sn_ref, k_ref, vt_ref, o_ref, qt_sc, s_sc, m_sc, l_sc, acc_sc, *, scale):
    nq = m_sc.shape[0]
    nblk, _, kvb = vt_ref.shape
    kgrp = kvb // SUBLANES
    vgrp = VDIM // SUBLANES
    qr = qr_ref[...].astype(F32)
    roped = qr * cs_ref[...] + _rope_partner(qr) * sn_ref[...]
    q = jnp.concatenate([qn_ref[...].astype(F32), roped], axis=1) * (scale * LOG2_E)
    qt_sc[...] = q.T.astype(BF16)
    m_sc[...] = jnp.full(m_sc.shape, -jnp.inf, F32)
    l_sc[...] = jnp.zeros(l_sc.shape, F32)
    acc_sc[...] = jnp.zeros(acc_sc.shape, F32)

    def scores(c, j):
        kc = k_ref[pl.ds(pl.multiple_of(c * kvb, kvb), kvb), :]
        s_sc[j] = _dot(kc, qt_sc[:, j * Q_BLOCK:(j + 1) * Q_BLOCK])

    for j in range(nq):
        scores(0, j)

    def body(c, carry):
        nxt = jnp.minimum(c + 1, nblk - 1)
        vtc = vt_ref[c]
        for j in range(nq):
            sc = s_sc[j].reshape(kgrp, SUBLANES, Q_BLOCK)
            m_prev = m_sc[j]
            m_new = jnp.maximum(m_prev, _sublane_all(jnp.max(sc, axis=0), jnp.maximum))
            alpha = jnp.exp2(m_prev - m_new)
            p = jnp.exp2(sc - m_new[None])
            l_sc[j] = alpha * l_sc[j] + jnp.sum(p, axis=0)
            pv = _dot(vtc, p.reshape(kvb, Q_BLOCK).astype(BF16))
            scores(nxt, j)
            acc = acc_sc[j].reshape(vgrp, SUBLANES, Q_BLOCK) * alpha[None]
            acc_sc[j] = acc.reshape(VDIM, Q_BLOCK) + pv
            m_sc[j] = m_new
        return carry

    lax.fori_loop(0, nblk, body, 0, unroll=min(KV_UNROLL, nblk))
    for j in range(nq):
        inv = 1.0 / _sublane_all(l_sc[j], jnp.add)
        out_t = (acc_sc[j].reshape(vgrp, SUBLANES, Q_BLOCK) * inv[None]).reshape(VDIM, Q_BLOCK)
        o_ref[j * Q_BLOCK:(j + 1) * Q_BLOCK, :] = out_t.T.astype(o_ref.dtype)


def _attention(u_main, k, vt, cs_tab, sn_tab, b, s):
    t = b * s
    tq = min(s, Q_SUBBLOCKS * Q_BLOCK)
    nq = tq // Q_BLOCK
    qpt = s // tq
    kvb = vt.shape[-1]
    tab = pl.BlockSpec((tq, LANES), lambda bb, h, qi: (bb * qpt + qi, 0))
    kern = functools.partial(_attn_kernel, scale=float(NOPE + ROPE) ** -0.5)
    return pl.pallas_call(
        kern,
        out_shape=jax.ShapeDtypeStruct((t, HEADS * VDIM), BF16),
        grid=(b, HEADS, qpt),
        in_specs=[pl.BlockSpec((tq, NOPE), lambda bb, h, qi: (bb * qpt + qi, h)),
                  pl.BlockSpec((tq, LANES), lambda bb, h, qi: (bb * qpt + qi, OFF_QR // LANES + h // 2)),
                  tab, tab,
                  pl.BlockSpec((None, None, s, 2 * LANES), lambda bb, h, qi: (bb, h, 0, 0)),
                  pl.BlockSpec((None, None, s // kvb, VDIM, kvb), lambda bb, h, qi: (bb, h, 0, 0, 0))],
        out_specs=pl.BlockSpec((tq, VDIM), lambda bb, h, qi: (bb * qpt + qi, h)),
        scratch_shapes=[pltpu.VMEM((2 * LANES, tq), BF16), pltpu.VMEM((nq, kvb, Q_BLOCK), F32),
                        pltpu.VMEM((nq, SUBLANES, Q_BLOCK), F32), pltpu.VMEM((nq, SUBLANES, Q_BLOCK), F32),
                        pltpu.VMEM((nq, VDIM, Q_BLOCK), F32)],
        compiler_params=_params("parallel", "parallel", "arbitrary"),
        name="mla_attention",
    )(u_main, u_main, cs_tab, sn_tab, k, vt)


HALO = BF16_ROWS
CONV_COLS = 512
assert all(o % CONV_COLS == 0 for o in (OFF_XBC, OFF_SCB, OFF_SCC, OFF_SCX))


def _fill_ext(ext_ref, prev, cur, nxt, first, last):
    tm = cur.shape[0]
    ext_ref[:HALO, :] = jnp.where(first, 0.0, prev.astype(F32))
    ext_ref[HALO:HALO + tm, :] = cur.astype(F32)
    ext_ref[HALO + tm:, :] = jnp.where(last, 0.0, nxt.astype(F32))


def _taps(ext_ref, w_ref, ksize, tm):
    pad = ksize // 2
    acc = None
    for kk in range(ksize):
        term = ext_ref[pl.ds(HALO - pad + kk, tm), :] * w_ref[kk:kk + 1, :]
        acc = term if acc is None else acc + term
    return acc


def _ssm_conv_kernel(prev_ref, cur_ref, next_ref, w_ref, b_ref, o_ref, ext_ref, *, tiles_per_seq):
    i = pl.program_id(0)
    tm = cur_ref.shape[0]
    _fill_ext(ext_ref, prev_ref[...], cur_ref[...], next_ref[...],
              i % tiles_per_seq == 0, i % tiles_per_seq == tiles_per_seq - 1)
    o_ref[...] = _silu(_taps(ext_ref, w_ref, SSM_CONV, tm) + b_ref[...]).astype(o_ref.dtype)


def _halo_specs(tm, tc, col0, n_rows):
    r = tm // HALO
    last = n_rows // HALO - 1
    cb = col0 // tc
    return [pl.BlockSpec((HALO, tc), lambda i, j: (jnp.maximum(i * r - 1, 0), cb + j)),
            pl.BlockSpec((tm, tc), lambda i, j: (i, cb + j)),
            pl.BlockSpec((HALO, tc), lambda i, j: (jnp.minimum((i + 1) * r, last), cb + j))]


def _ssm_conv(u_main, conv_w, conv_b, s):
    t = u_main.shape[0]
    tm = min(s, 512)
    tc = CONV_COLS
    kern = functools.partial(_ssm_conv_kernel, tiles_per_seq=s // tm)
    return pl.pallas_call(
        kern,
        out_shape=jax.ShapeDtypeStruct((t, XBC), BF16),
        grid=(t // tm, XBC // tc),
        in_specs=_halo_specs(tm, tc, OFF_XBC, t) + [
            pl.BlockSpec((SSM_CONV, tc), lambda i, j: (0, j)),
            pl.BlockSpec((1, tc), lambda i, j: (0, j))],
        out_specs=pl.BlockSpec((tm, tc), lambda i, j: (i, j)),
        scratch_shapes=[pltpu.VMEM((tm + 2 * HALO, tc), F32)],
        compiler_params=_params("parallel", "parallel"),
        name="ssm_conv",
    )(u_main, u_main, u_main, conv_w, conv_b.reshape(1, XBC))


def _sconv_kernel(cp_ref, cc_ref, cn_ref, xp_ref, xc_ref, xn_ref, b_ref, w_ref, o_ref, ext_ref, *, tiles_per_seq):
    i = pl.program_id(0)
    tm = cc_ref.shape[0]
    f = lambda r: r[...].astype(F32)
    _fill_ext(ext_ref, f(cp_ref) * f(xp_ref), f(cc_ref) * f(xc_ref), f(cn_ref) * f(xn_ref),
              i % tiles_per_seq == 0, i % tiles_per_seq == tiles_per_seq - 1)
    o_ref[...] = (f(b_ref) * _taps(ext_ref, w_ref, SCONV_K, tm)).astype(o_ref.dtype)


def _short_conv(u_main, sconv_w, s):
    t = u_main.shape[0]
    tm = min(s, 512)
    tc = CONV_COLS
    kern = functools.partial(_sconv_kernel, tiles_per_seq=s // tm)
    return pl.pallas_call(
        kern,
        out_shape=jax.ShapeDtypeStruct((t, D), BF16),
        grid=(t // tm, D // tc),
        in_specs=_halo_specs(tm, tc, OFF_SCC, t) + _halo_specs(tm, tc, OFF_SCX, t) + [
            pl.BlockSpec((tm, tc), lambda i, j: (i, OFF_SCB // tc + j)),
            pl.BlockSpec((SCONV_K, tc), lambda i, j: (0, j))],
        out_specs=pl.BlockSpec((tm, tc), lambda i, j: (i, j)),
        scratch_shapes=[pltpu.VMEM((tm + 2 * HALO, tc), F32)],
        compiler_params=_params("parallel", "parallel"),
        name="short_conv",
    )(u_main, u_main, u_main, u_main, u_main, u_main, u_main, sconv_w)


FWD_COL = 2 * SSM_HEADS
BWD_COL = 3 * SSM_HEADS
PAIRS = SSM_HEADS // 2
PAIRS_PER_GROUP = PAIRS // SSM_GROUPS


def _cumsum(x, axis, reverse):
    n = x.shape[axis]
    idx = lax.broadcasted_iota(I32, x.shape, axis)
    k = 1
    while k < n:
        if reverse:
            x = x + jnp.where(idx < n - k, pltpu.roll(x, n - k, axis), 0.0)
        else:
            x = x + jnp.where(idx >= k, pltpu.roll(x, k, axis), 0.0)
        k *= 2
    return x


def _spread_heads(vals, sel, base, exact):
    lane = _lane_iota(vals.shape)
    v = jnp.where(lane >= base, jnp.where(lane < base + SSM_HEADS, vals, 0.0), 0.0)
    hi = v.astype(BF16)
    out = _dot(hi, sel)
    if exact:
        rest = v - hi.astype(F32)
        mid = rest.astype(BF16)
        out = out + _dot(mid, sel) + _dot((rest - mid.astype(F32)).astype(BF16), sel)
    return out


def _ssd_direction(xbc_ref, us_ref, ust_ref, sel, dtb_row, a_row, dtb_col, a_col, d_row, y_ref, st_ref, *, reverse):
    base = BWD_COL if reverse else FWD_COL
    edge = 0 if reverse else CHUNK - 1
    dt_c = _softplus(us_ref[...] + dtb_row)
    cum_c = _cumsum(dt_c * a_row, 0, reverse)
    dt_r = _softplus(ust_ref[...] + dtb_col)
    cum_r = _cumsum(dt_r * a_col, 1, reverse)
    tot = cum_c[edge:edge + 1, :]
    dt_x = _spread_heads(dt_c, sel, base, False)
    ecum_x = _spread_heads(jnp.exp(cum_c), sel, base, False)
    dec_x = _spread_heads(jnp.exp(tot - cum_c) * dt_c, sel, base, False)
    etot_x = _spread_heads(jnp.broadcast_to(jnp.exp(tot), (SUBLANES, LANES)), sel, base, True)[0:1, :]

    li = lax.broadcasted_iota(I32, (CHUNK, CHUNK), 0)
    si = lax.broadcasted_iota(I32, (CHUNK, CHUNK), 1)
    mask = (li <= si) if reverse else (li >= si)
    lane = _lane_iota((CHUNK, LANES))
    lo = lane < SSM_P

    for g in range(SSM_GROUPS):
        bg = xbc_ref[:, SSM_INNER + g * SSM_N:SSM_INNER + (g + 1) * SSM_N]
        cg = xbc_ref[:, SSM_INNER + (SSM_GROUPS + g) * SSM_N:SSM_INNER + (SSM_GROUPS + g + 1) * SSM_N]
        cb = _dot_nt(cg, bg)
        for j in range(PAIRS_PER_GROUP):
            pr = g * PAIRS_PER_GROUP + j
            c1, c2 = base + 2 * pr, base + 2 * pr + 1
            cols = slice(pr * LANES, (pr + 1) * LANES)

            xs = xbc_ref[:, cols].astype(F32)
            xdt = xs * dt_x[:, cols]
            scores = []
            for c in (c1, c2):
                seg = cum_c[:, c:c + 1] - cum_r[c:c + 1, :]
                scores.append((cb * jnp.exp(jnp.where(mask, seg, -jnp.inf))).astype(BF16))
            xb = xdt.astype(BF16)
            zero = jnp.zeros_like(xb)
            x_bd = jnp.concatenate([jnp.where(lo, xb, zero), jnp.where(lo, zero, xb)], axis=0)
            y = _dot(jnp.concatenate(scores, axis=1), x_bd)
            st = st_ref[g, :, j * LANES:(j + 1) * LANES]
            y = y + _dot(cg, st.astype(BF16)) * ecum_x[:, cols]
            if not reverse:
                y = y + xs * d_row[:, cols]
            y_ref[:, cols] = y.astype(y_ref.dtype)
            new = _dot_tn(bg, (xs * dec_x[:, cols]).astype(BF16))
            st_ref[g, :, j * LANES:(j + 1) * LANES] = st * etot_x[:, cols] + new


def _ssd_kernel(xf_ref, usf_ref, ustf_ref, xb_ref, usb_ref, ustb_ref, sel_ref, dtb_row_ref, alog_row_ref,
                dtb_col_ref, alog_col_ref, d_ref, yf_ref, yb_ref, stf_ref, stb_ref):
    @pl.when(pl.program_id(1) == 0)
    def _():
        stf_ref[...] = jnp.zeros(stf_ref.shape, F32)
        stb_ref[...] = jnp.zeros(stb_ref.shape, F32)

    dtb_row, dtb_col = dtb_row_ref[...], dtb_col_ref[...]
    a_row, a_col = -jnp.exp(alog_row_ref[...]), -jnp.exp(alog_col_ref[...])
    d_row = d_ref[...]
    sel = sel_ref[...]
    _ssd_direction(xf_ref, usf_ref, ustf_ref, sel, dtb_row, a_row, dtb_col, a_col, d_row, yf_ref, stf_ref,
                   reverse=False)
    _ssd_direction(xb_ref, usb_ref, ustb_ref, sel, dtb_row, a_row, dtb_col, a_col, d_row, yb_ref, stb_ref,
                   reverse=True)


def _ssd_scan(xbc_conv, u_small, u_small_t, a_log, dt_bias, d_skip, b, s):
    t = b * s
    nc = s // CHUNK
    pad = jnp.zeros((2 * SSM_HEADS,), F32)
    dtb = jnp.concatenate([pad, dt_bias.reshape(-1)])
    alog = jnp.concatenate([pad, a_log.reshape(-1)])
    d_row = jnp.repeat(d_skip, SSM_P)[None, :]
    head_of_lane = np.concatenate([np.full(2 * SSM_HEADS, -1), np.arange(SSM_HEADS), np.arange(SSM_HEADS)])
    sel = jnp.asarray(head_of_lane[:, None] == np.repeat(np.arange(SSM_HEADS), SSM_P)[None, :], BF16)
    fw = lambda bb, i: bb * nc + i
    bw = lambda bb, i: bb * nc + nc - 1 - i
    row = lambda w: pl.BlockSpec((1, w), lambda bb, i: (0, 0))
    col = pl.BlockSpec((LANES, 1), lambda bb, i: (0, 0))

    def chunk_specs(idx):
        return [pl.BlockSpec((CHUNK, XBC), lambda bb, i: (idx(bb, i), 0)),
                pl.BlockSpec((CHUNK, LANES), lambda bb, i: (idx(bb, i), 0)),
                pl.BlockSpec((LANES, CHUNK), lambda bb, i: (0, idx(bb, i)))]

    return pl.pallas_call(
        _ssd_kernel,
        out_shape=(jax.ShapeDtypeStruct((t, SSM_INNER), BF16),) * 2,
        grid=(b, nc),
        in_specs=chunk_specs(fw) + chunk_specs(bw) + [
            pl.BlockSpec((LANES, SSM_INNER), lambda bb, i: (0, 0)),
            row(LANES), row(LANES), col, col, row(SSM_INNER)],
        out_specs=(pl.BlockSpec((CHUNK, SSM_INNER), lambda bb, i: (fw(bb, i), 0)),
                   pl.BlockSpec((CHUNK, SSM_INNER), lambda bb, i: (bw(bb, i), 0))),
        scratch_shapes=[pltpu.VMEM((SSM_GROUPS, SSM_N, SSM_INNER // SSM_GROUPS), F32)] * 2,
        compiler_params=_params("parallel", "arbitrary"),
        name="ssd_scan",
    )(xbc_conv, u_small, u_small_t, xbc_conv, u_small, u_small_t, sel,
      dtb[None, :], alog[None, :], dtb[:, None], alog[:, None], d_row)


def _ssm_post_kernel(yf_ref, yb_ref, z_ref, w_ref, o_ref):
    y = (yf_ref[...].astype(F32) + yb_ref[...].astype(F32)) * _silu(z_ref[...].astype(F32))
    o_ref[...] = _rms(y, w_ref[...]).astype(o_ref.dtype)


def _ssm_post(y_fwd, y_bwd, u_main, norm_w):
    t = y_fwd.shape[0]
    tm = min(t, 512)
    gw = SSM_INNER // SSM_GROUPS
    blk = pl.BlockSpec((tm, gw), lambda i, g: (i, g))
    return pl.pallas_call(
        _ssm_post_kernel,
        out_shape=jax.ShapeDtypeStruct((t, SSM_INNER), BF16),
        grid=(t // tm, SSM_GROUPS),
        in_specs=[blk, blk, pl.BlockSpec((tm, gw), lambda i, g: (i, OFF_Z // gw + g)),
                  pl.BlockSpec((1, gw), lambda i, g: (0, g))],
        out_specs=blk,
        compiler_params=_params("parallel", "parallel"),
        name="ssm_gate_norm",
    )(y_fwd, y_bwd, u_main, norm_w.reshape(1, SSM_INNER))


def _merge_kernel(ya_ref, ys_ref, yc_ref, pa_ref, ps_ref, pc_ref, ga_ref, gs_ref, gc_ref, o_ref):
    g = lambda r: _sigmoid(r[...].astype(F32))
    m = (g(ga_ref) * _dot(ya_ref[...], pa_ref[...]) + g(gs_ref) * _dot(ys_ref[...], ps_ref[...])
         + g(gc_ref) * _dot(yc_ref[...], pc_ref[...]))
    o_ref[...] = m.astype(o_ref.dtype)


def _merge(y_att, y_ssm, y_conv, w_branch, u_main):
    t = y_att.shape[0]
    tm = min(t, 512)
    tn = 512
    yb = pl.BlockSpec((tm, D), lambda i, j: (i, 0))
    pspec = lambda r: pl.BlockSpec((D, tn), lambda i, j: (r, j))
    gspec = lambda r: pl.BlockSpec((tm, tn), lambda i, j: (i, OFF_GATE // tn + r * (D // tn) + j))
    return pl.pallas_call(
        _merge_kernel,
        out_shape=jax.ShapeDtypeStruct((t, D), BF16),
        grid=(t // tm, D // tn),
        in_specs=[yb, yb, yb, pspec(0), pspec(1), pspec(2), gspec(0), gspec(1), gspec(2)],
        out_specs=pl.BlockSpec((tm, tn), lambda i, j: (i, j)),
        compiler_params=_params("parallel", "arbitrary"),
        name="branch_merge",
    )(y_att, y_ssm, y_conv, w_branch, w_branch, w_branch, u_main, u_main, u_main)


def _split_bf16(x):
    hi = x.astype(BF16)
    lo = (x - hi.astype(F32)).astype(BF16)
    return hi, lo


def _wo_kernel(m_ref, h_ref, wo_ref, nw_ref, rhi_ref, rlo_ref, rb_ref, h1_ref, t_ref, lg_ref):
    h1 = h_ref[...] + _dot(m_ref[...], wo_ref[...])
    h1_ref[...] = h1
    tn = _rms(h1, nw_ref[...])
    for sl in range(SLAB):
        t_ref[pl.ds(sl, tn.shape[0], stride=SLAB), :] = tn[:, sl * LANES:(sl + 1) * LANES]
    hi, lo = _split_bf16(tn)
    lg_ref[...] = _dot(hi, rhi_ref[...]) + _dot(lo, rhi_ref[...]) + _dot(hi, rlo_ref[...]) + rb_ref[...]


def _out_proj(merged, h, w_o, ffn_norm_w, r_hi, r_lo, r_bias):
    t = h.shape[0]
    tm = min(t, 256)
    blk = pl.BlockSpec((tm, D), lambda i: (i, 0))
    const = lambda shape: pl.BlockSpec(shape, lambda i: (0, 0))
    lg = pl.BlockSpec((tm, LANES), lambda i: (i, 0))
    return pl.pallas_call(
        _wo_kernel,
        out_shape=(jax.ShapeDtypeStruct((t, D), F32), jax.ShapeDtypeStruct((t * SLAB, LANES), F32),
                   jax.ShapeDtypeStruct((t, LANES), F32)),
        grid=(t // tm,),
        in_specs=[blk, blk, const((D, D)), const((1, D)), const((D, LANES)), const((D, LANES)),
                  const((1, LANES))],
        out_specs=(blk, pl.BlockSpec((tm * SLAB, LANES), lambda i: (i, 0)), lg),
        compiler_params=_params("parallel"),
        name="out_proj_router",
    )(merged, h, w_o, ffn_norm_w.reshape(1, D), r_hi, r_lo, r_bias)


def _route_kernel(lg_ref, id_ref, wt_ref, rk_ref, cnt_ref, carry_sc):
    @pl.when(pl.program_id(0) == 0)
    def _():
        carry_sc[...] = jnp.zeros(carry_sc.shape, F32)

    lg = lg_ref[...]
    lane = _lane_iota(lg.shape).astype(F32)
    big = float(LANES)
    ninf = -jnp.inf
    first = lambda hit: jnp.min(jnp.where(hit, lane, big), axis=1, keepdims=True)

    gl = jnp.where(lane < N_GROUPS, lg, ninf)
    gmax = jnp.max(gl, axis=1, keepdims=True)
    gidx = first(gl == gmax)
    g_w = 1.0 / jnp.sum(jnp.exp(gl - gmax), axis=1, keepdims=True)

    in_group = (lane >= N_GROUPS + gidx * EPG) & (lane < N_GROUPS + (gidx + 1.0) * EPG)
    el = jnp.where(in_group, lg, ninf)
    v0 = jnp.max(el, axis=1, keepdims=True)
    i0 = first(el == v0)
    el = jnp.where(lane == i0, ninf, el)
    v1 = jnp.max(el, axis=1, keepdims=True)
    i1 = first(el == v1)
    e1 = jnp.exp(v1 - v0)
    w0 = g_w / (1.0 + e1)
    w1 = g_w * e1 / (1.0 + e1)
    id_ref[...] = jnp.where(lane == 0.0, i0, i1).astype(I32) - N_GROUPS
    wt_ref[...] = jnp.where(lane == 0.0, w0, w1)

    hit0 = lane == i0 - N_GROUPS
    hit1 = lane == i1 - N_GROUPS
    onehot = jnp.where(hit0, 1.0, jnp.where(hit1, 1.0, 0.0))
    tm = lg.shape[0]
    earlier = lax.broadcasted_iota(I32, (tm, tm), 1) < lax.broadcasted_iota(I32, (tm, tm), 0)
    carry = carry_sc[0:1, :]
    before = _dot(jnp.where(earlier, 1.0, 0.0).astype(BF16), onehot.astype(BF16)) + carry
    r0 = jnp.sum(jnp.where(hit0, before, 0.0), axis=1, keepdims=True)
    r1 = jnp.sum(jnp.where(hit1, before, 0.0), axis=1, keepdims=True)
    rk_ref[...] = jnp.where(lane == 0.0, r0, r1).astype(I32)
    carry_sc[...] = jnp.broadcast_to(carry + jnp.sum(onehot, axis=0, keepdims=True), carry_sc.shape)
    cnt_ref[...] = carry_sc[...]


def _route(logits):
    t = logits.shape[0]
    tm = min(t, 512)
    blk = pl.BlockSpec((tm, LANES), lambda i: (i, 0))
    cnt = pl.BlockSpec((SUBLANES, LANES), lambda i: (0, 0))
    return pl.pallas_call(
        _route_kernel,
        out_shape=(jax.ShapeDtypeStruct((t, LANES), I32), jax.ShapeDtypeStruct((t, LANES), F32),
                   jax.ShapeDtypeStruct((t, LANES), I32), jax.ShapeDtypeStruct((SUBLANES, LANES), F32)),
        grid=(t // tm,),
        in_specs=[blk],
        out_specs=(blk, blk, blk, cnt),
        scratch_shapes=[pltpu.VMEM((SUBLANES, LANES), F32)],
        compiler_params=_params("arbitrary"),
        name="route",
    )(logits)


def _dispatch_plan(ids, wts, ranks, counts, t, tile):
    n_rows = 2 * t
    n_tiles = n_rows // tile + N_EXPERTS
    n_pos = n_tiles * tile
    e = ids[:, :2].reshape(-1)
    w = wts[:, :2].reshape(-1)
    rank = ranks[:, :2].reshape(-1)
    counts = counts[0, :N_EXPERTS].astype(I32)
    padded = ((counts + tile - 1) // tile) * tile
    ends = jnp.cumsum(padded)
    starts = ends - padded
    pos = starts[e] + rank
    r = jnp.arange(n_rows, dtype=I32)
    packed = jnp.stack([r, lax.bitcast_convert_type(w, I32)], axis=1)
    slots = jnp.full((n_pos, 2), -1, I32).at[pos].set(packed)
    src = slots[:, 0]
    valid = src >= 0
    tile_start = jnp.arange(n_tiles, dtype=I32) * tile
    tile_e = jnp.minimum(jnp.sum(tile_start[:, None] >= ends[None, :], axis=1), N_EXPERTS - 1).astype(I32)
    in_tile = jnp.arange(n_pos, dtype=I32) % tile
    row_tok = jnp.where(valid, src // 2, 0)
    row_dst = jnp.where(valid, (src % 2) * t + src // 2, n_rows + in_tile)
    row_w = jnp.where(valid, lax.bitcast_convert_type(slots[:, 1], F32), 0.0)
    n_used = (ends[-1] // tile).astype(I32).reshape(1)
    return (row_tok.reshape(n_tiles, 1, tile), row_dst.reshape(n_tiles, 1, tile),
            row_w.reshape(n_pos, 1), tile_e, n_used)


DMA_UNROLL = 8


def _moe_kernel(te_ref, nu_ref, tok_ref, tokn_ref, dstp_ref, dst_ref, rw_ref, t_hbm, wg_ref, wu_ref, wd_ref, o_hbm,
                xbuf, ybuf, wg_sc, wu_sc, wd_sc, gsem, ssem):
    i = pl.program_id(0)
    n_used = nu_ref[0]
    tile = rw_ref.shape[0]
    rows = tile * PITCH
    slot = i % 2

    def gather_copy(tok, r, slot_):
        dst = xbuf.at[pl.ds(pl.multiple_of(slot_ * rows + r * PITCH, SUBLANES), SLAB), :]
        return pltpu.make_async_copy(t_hbm.at[tok[0, 0, r]], dst, gsem.at[slot_])

    def scatter_copy(dst, r):
        src = ybuf.at[pl.ds(pl.multiple_of(r * PITCH, SUBLANES), SLAB), :]
        return pltpu.make_async_copy(src, o_hbm.at[dst[0, 0, r]], ssem)

    def wait_rows(buf, sem):
        done = buf.at[pl.ds(0, tile * SLAB), :]
        pltpu.make_async_copy(done, done, sem).wait()

    def looped(make):
        def body(r, c):
            make(r).start()
            return c
        lax.fori_loop(0, tile, body, 0, unroll=DMA_UNROLL)

    @pl.when(i == 0)
    def _():
        ybuf[...] = jnp.zeros(ybuf.shape, F32)
        looped(lambda r: gather_copy(tok_ref, r, 0))

    @pl.when(i < n_used)
    def _():
        @pl.when((i == 0) | (te_ref[i] != te_ref[jnp.maximum(i - 1, 0)]))
        def _():
            wg_sc[...] = wg_ref[...].astype(BF16)
            wu_sc[...] = wu_ref[...].astype(BF16)
            wd_sc[...] = wd_ref[...].astype(BF16)

        wait_rows(xbuf, gsem.at[slot])
        base = slot * rows
        x = jnp.concatenate([xbuf[pl.ds(base + sl, tile, stride=PITCH), :] for sl in range(SLAB)],
                            axis=1).astype(BF16)
        for r in range(tile):
            gather_copy(tokn_ref, r, 1 - slot).start()
        for r in range(tile):
            scatter_copy(dstp_ref, r).start()
        gate = _dot(x, wg_sc[...])
        mid = (_silu(gate) * _dot(x, wu_sc[...])).astype(BF16)
        y = _dot(mid, wd_sc[...]) * rw_ref[...]
        wait_rows(ybuf, ssem)
        for sl in range(SLAB):
            ybuf[pl.ds(sl, tile, stride=PITCH), :] = y[:, sl * LANES:(sl + 1) * LANES]

        @pl.when(i == n_used - 1)
        def _():
            looped(lambda r: scatter_copy(dst_ref, r))
            wait_rows(ybuf, ssem)
            wait_rows(xbuf, gsem.at[1 - slot])


def _moe(t_norm, plan, layer, w_gate, w_up, w_down):
    t = t_norm.shape[0] // SLAB
    row_tok, row_dst, row_w, tile_e, n_used = plan
    n_tiles, _, tile = row_tok.shape
    t_slab = t_norm.reshape(t, SLAB, LANES)
    extra = (2 * t + jnp.arange(tile, dtype=I32)).reshape(1, 1, tile)
    dst_prev = jnp.concatenate([extra, row_dst[:-1]], axis=0)
    smem = lambda f: pl.BlockSpec((1, 1, tile), lambda i, te, nu: (f(i), 0, 0), memory_space=pltpu.SMEM)
    wspec = lambda a, b: pl.BlockSpec((None, None, a, b), lambda i, te, nu: (layer, te[i], 0, 0))
    out = pl.pallas_call(
        _moe_kernel,
        out_shape=jax.ShapeDtypeStruct((2 * t + tile, SLAB, LANES), F32),
        grid_spec=pltpu.PrefetchScalarGridSpec(
            num_scalar_prefetch=2,
            grid=(n_tiles,),
            in_specs=[smem(lambda i: i), smem(lambda i: jnp.minimum(i + 1, n_tiles - 1)),
                      smem(lambda i: i), smem(lambda i: i),
                      pl.BlockSpec((tile, 1), lambda i, te, nu: (i, 0)),
                      pl.BlockSpec(memory_space=pl.ANY),
                      wspec(D, D_EXPERT), wspec(D, D_EXPERT), wspec(D_EXPERT, D)],
            out_specs=pl.BlockSpec(memory_space=pl.ANY),
            scratch_shapes=[pltpu.VMEM((2 * tile * PITCH, LANES), F32), pltpu.VMEM((tile * PITCH, LANES), F32),
                            pltpu.VMEM((D, D_EXPERT), BF16), pltpu.VMEM((D, D_EXPERT), BF16),
                            pltpu.VMEM((D_EXPERT, D), BF16),
                            pltpu.SemaphoreType.DMA((2,)), pltpu.SemaphoreType.DMA(())]),
        compiler_params=_params("arbitrary"),
        name="moe_experts",
    )(tile_e, n_used, row_tok, row_tok, dst_prev, row_dst, row_w, t_slab, w_gate, w_up, w_down)
    return out.reshape((2 * t + tile) * SLAB, LANES)


def kernel(x, positions, attn_norm_w, w_in, kv_norm_w, w_uk, w_uv, ssm_conv_w, ssm_conv_b, ssm_A_log, ssm_dt_bias, ssm_D, ssm_norm_w, sconv_w, w_branch, w_o, ffn_norm_w, router_group_w, router_group_b, router_expert_w, router_expert_b, expert_w_gate, expert_w_up, expert_w_down, final_norm_w):
    b, s, _ = x.shape
    t = b * s
    depth = w_in.shape[0]
    h = x.reshape(t, D)
    cs_tab, sn_tab = _rope_tables(positions)
    moe_out = None
    for l in range(depth):
        if moe_out is None:
            n = _norm(h, attn_norm_w[l], BF16)
        else:
            h, n = _add_norm(h, moe_out, attn_norm_w[l], BF16)
        w_perm = _permute_in_proj(w_in[l])
        w_small = w_perm[:, OFF_SMALL:OFF_SMALL + LANES]
        u_main, u_small, u_small_t = _in_proj(n, w_perm, w_small, w_small.T)

        k, vt = _kv_proj(u_main, u_small, kv_norm_w[l], w_uk[l].astype(BF16), w_uv[l].T.astype(BF16),
                         cs_tab, sn_tab, b, s)
        y_att = _attention(u_main, k, vt, cs_tab, sn_tab, b, s)

        xbc_conv = _ssm_conv(u_main, ssm_conv_w[l], ssm_conv_b[l], s)
        y_fwd, y_bwd = _ssd_scan(xbc_conv, u_small, u_small_t, ssm_A_log[l], ssm_dt_bias[l], ssm_D[l], b, s)
        y_ssm = _ssm_post(y_fwd, y_bwd, u_main, ssm_norm_w[l])

        y_conv = _short_conv(u_main, sconv_w[l], s)

        merged = _merge(y_att, y_ssm, y_conv, w_branch[l].astype(BF16), u_main)

        r_w = jnp.concatenate([router_group_w[l], router_expert_w[l],
                               jnp.zeros((D, LANES - N_GROUPS - N_EXPERTS), F32)], axis=1)
        r_b = jnp.concatenate([router_group_b[l], router_expert_b[l],
                               jnp.zeros((LANES - N_GROUPS - N_EXPERTS,), F32)])[None, :]
        r_hi, r_lo = _split_bf16(r_w)
        h, t_norm, logits = _out_proj(merged, h, w_o[l].astype(BF16), ffn_norm_w[l], r_hi, r_lo, r_b)

        ids, wts, ranks, counts = _route(logits)
        plan = _dispatch_plan(ids, wts, ranks, counts, t, MOE_TILE)
        moe_out = _moe(t_norm, plan, l, expert_w_gate, expert_w_up, expert_w_down)

    h, out = _add_norm(h, moe_out, final_norm_w, F32)
    return out.reshape(b, s, D)
```

```python
import functools

import numpy as np
import jax
import jax.numpy as jnp
from jax import lax
from jax.experimental import pallas as pl
from jax.experimental.pallas import tpu as pltpu

F32 = jnp.float32
BF16 = jnp.bfloat16
I32 = jnp.int32

D = 2048
EPS = 1e-6
LOG2_E = 1.4426950408889634
HEADS = 16
NOPE = 128
ROPE = 64
VDIM = 128
KV_RANK = 512
ROPE_THETA = 10000.0
SSM_HEADS = 32
SSM_P = 64
SSM_INNER = SSM_HEADS * SSM_P
SSM_GROUPS = 4
SSM_N = 128
SSM_CONV = 5
CHUNK = 128
XBC = SSM_INNER + 2 * SSM_GROUPS * SSM_N
SCONV_K = 3
N_GROUPS = 8
EPG = 4
N_EXPERTS = N_GROUPS * EPG
D_EXPERT = 512
Q_DIM = HEADS * (NOPE + ROPE)
IN_COLS = Q_DIM + KV_RANK + ROPE + SSM_INNER + XBC + 2 * SSM_HEADS + 3 * D + 3 * D

LANES = 128
SUBLANES = 8
BF16_ROWS = 16
V7X_VMEM_BYTES = 64 * 1024 * 1024
VMEM_LIMIT = 56 * 1024 * 1024

OFF_QN = 0
OFF_QR = OFF_QN + HEADS * NOPE
OFF_CKV = OFF_QR + HEADS * ROPE
OFF_Z = OFF_CKV + KV_RANK
OFF_XBC = OFF_Z + SSM_INNER
OFF_SCB = OFF_XBC + XBC
OFF_SCC = OFF_SCB + D
OFF_SCX = OFF_SCC + D
OFF_GATE = OFF_SCX + D
OFF_SMALL = OFF_GATE + 3 * D
assert OFF_SMALL + LANES == IN_COLS
IN_PROJ_TN = 1024
U_COLS = -(-IN_COLS // IN_PROJ_TN) * IN_PROJ_TN

SLAB = D // LANES
PITCH = 24
MOE_TILE = 384


def _permute_in_proj(w):
    o_ckv = Q_DIM
    o_kr = o_ckv + KV_RANK
    o_z = o_kr + ROPE
    o_dt = o_z + SSM_INNER + XBC
    o_rest = o_dt + 2 * SSM_HEADS
    q = w[:, :Q_DIM].reshape(D, HEADS, NOPE + ROPE)
    parts = [q[:, :, :NOPE].reshape(D, HEADS * NOPE), q[:, :, NOPE:].reshape(D, HEADS * ROPE),
             w[:, o_ckv:o_kr], w[:, o_z:o_dt], w[:, o_rest:], w[:, o_kr:o_z], w[:, o_dt:o_rest]]
    parts.append(jnp.zeros((D, U_COLS - IN_COLS), w.dtype))
    out = jnp.concatenate([p.astype(BF16) for p in parts], axis=1)
    assert out.shape == (D, U_COLS)
    return out


def _params(*sem):
    return pltpu.CompilerParams(dimension_semantics=tuple(sem), vmem_limit_bytes=VMEM_LIMIT)


def _dot(a, b):
    return jnp.dot(a, b, preferred_element_type=F32)


def _dot_nt(a, b):
    return lax.dot_general(a, b, (((1,), (1,)), ((), ())), preferred_element_type=F32)


def _dot_tn(a, b):
    return lax.dot_general(a, b, (((0,), (0,)), ((), ())), preferred_element_type=F32)


def _sigmoid(x):
    return 1.0 / (1.0 + jnp.exp(-x))


def _silu(x):
    return x * _sigmoid(x)


def _softplus(x):
    return jnp.maximum(x, 0.0) + jnp.log(1.0 + jnp.exp(-jnp.abs(x)))


def _lane_iota(shape):
    return lax.broadcasted_iota(I32, shape, len(shape) - 1)


def _rope_partner(v):
    lane = _lane_iota(v.shape)
    return jnp.where((lane & 32) == 0, pltpu.roll(v, LANES - 32, 1), pltpu.roll(v, 32, 1))


def _rope_tab_kernel(pos_ref, freq_ref, sign_ref, cs_ref, sn_ref):
    ang = pos_ref[...].astype(F32) * freq_ref[...]
    cs_ref[...] = jnp.cos(ang)
    sn_ref[...] = jnp.sin(ang) * sign_ref[...]


def _rope_tables(positions):
    t = positions.size
    tm = min(t, 1024)
    freqs = ROPE_THETA ** (-jnp.arange(0, ROPE, 2, dtype=F32) / ROPE)
    freq_row = jnp.tile(freqs, 4)[None, :]
    sign_row = jnp.tile(jnp.concatenate([-jnp.ones(32, F32), jnp.ones(32, F32)]), 2)[None, :]
    row = pl.BlockSpec((1, LANES), lambda i: (0, 0))
    tab = pl.BlockSpec((tm, LANES), lambda i: (i, 0))
    return pl.pallas_call(
        _rope_tab_kernel,
        out_shape=(jax.ShapeDtypeStruct((t, LANES), F32),) * 2,
        grid=(t // tm,),
        in_specs=[pl.BlockSpec((tm, 1), lambda i: (i, 0)), row, row],
        out_specs=(tab, tab),
        compiler_params=_params("parallel"),
        name="rope_tables",
    )(positions.reshape(t, 1), freq_row, sign_row)


def _rms(x, w):
    return x * lax.rsqrt(jnp.mean(x * x, axis=-1, keepdims=True) + EPS) * w


def _norm_kernel(h_ref, w_ref, n_ref):
    n_ref[...] = _rms(h_ref[...], w_ref[...]).astype(n_ref.dtype)


def _rows_from_slabs(ref, tm):
    return jnp.concatenate([ref[pl.ds(sl, tm, stride=SLAB), :] for sl in range(SLAB)], axis=1)


def _add_norm_kernel(h_ref, o0_ref, o1_ref, w_ref, hs_ref, n_ref):
    tm = h_ref.shape[0]
    hs = h_ref[...] + _rows_from_slabs(o0_ref, tm) + _rows_from_slabs(o1_ref, tm)
    hs_ref[...] = hs
    n_ref[...] = _rms(hs, w_ref[...]).astype(n_ref.dtype)


def _norm(h, w, out_dtype):
    t = h.shape[0]
    tm = min(t, 256)
    blk = pl.BlockSpec((tm, D), lambda i: (i, 0))
    return pl.pallas_call(
        _norm_kernel,
        out_shape=jax.ShapeDtypeStruct((t, D), out_dtype),
        grid=(t // tm,),
        in_specs=[blk, pl.BlockSpec((1, D), lambda i: (0, 0))],
        out_specs=blk,
        compiler_params=_params("parallel"),
        name="rmsnorm",
    )(h, w.reshape(1, D))


def _add_norm(h, o, w, out_dtype):
    t = h.shape[0]
    tm = min(t, 256)
    nt = t // tm
    blk = pl.BlockSpec((tm, D), lambda i: (i, 0))
    return pl.pallas_call(
        _add_norm_kernel,
        out_shape=(jax.ShapeDtypeStruct((t, D), F32), jax.ShapeDtypeStruct((t, D), out_dtype)),
        grid=(nt,),
        in_specs=[blk, pl.BlockSpec((tm * SLAB, LANES), lambda i: (i, 0)),
                  pl.BlockSpec((tm * SLAB, LANES), lambda i: (nt + i, 0)),
                  pl.BlockSpec((1, D), lambda i: (0, 0))],
        out_specs=(blk, blk),
        compiler_params=_params("parallel"),
        name="add_rmsnorm",
    )(h, o, o, w.reshape(1, D))


def _inproj_kernel(n_ref, wm_ref, ws_ref, wst_ref, um_ref, us_ref, ust_ref):
    x = n_ref[...]
    um_ref[...] = _dot(x, wm_ref[...]).astype(um_ref.dtype)

    @pl.when(pl.program_id(1) == 0)
    def _():
        us_ref[...] = _dot(x, ws_ref[...])
        ust_ref[...] = _dot_nt(wst_ref[...], x)


def _in_proj(n, w_main, w_small, w_small_t):
    t = n.shape[0]
    tm = min(t, 1024)
    tn = IN_PROJ_TN
    return pl.pallas_call(
        _inproj_kernel,
        out_shape=(jax.ShapeDtypeStruct((t, U_COLS), BF16),
                   jax.ShapeDtypeStruct((t, LANES), F32),
                   jax.ShapeDtypeStruct((LANES, t), F32)),
        grid=(t // tm, U_COLS // tn),
        in_specs=[pl.BlockSpec((tm, D), lambda i, j: (i, 0)),
                  pl.BlockSpec((D, tn), lambda i, j: (0, j)),
                  pl.BlockSpec((D, LANES), lambda i, j: (0, 0)),
                  pl.BlockSpec((LANES, D), lambda i, j: (0, 0))],
        out_specs=(pl.BlockSpec((tm, tn), lambda i, j: (i, j)),
                   pl.BlockSpec((tm, LANES), lambda i, j: (i, 0)),
                   pl.BlockSpec((LANES, tm), lambda i, j: (0, i))),
        compiler_params=_params("parallel", "arbitrary"),
        name="in_proj",
    )(n, w_main, w_small, w_small_t)


KV_BLOCK = 1024


def _kv_kernel(ckv_ref, nw_ref, wuk_ref, wuvt_ref, us_ref, cs_ref, sn_ref, k_ref, vt_ref):
    nblk, kvb = vt_ref.shape[1], vt_ref.shape[3]
    c = _rms(ckv_ref[...].astype(F32), nw_ref[...]).astype(BF16)
    kn = _dot(c, wuk_ref[...]).astype(k_ref.dtype)
    vt = _dot_nt(wuvt_ref[...], c).astype(vt_ref.dtype)
    u = us_ref[...]
    roped = u * cs_ref[...] + _rope_partner(u) * sn_ref[...]
    lane = _lane_iota(roped.shape)
    even = jnp.where(lane < ROPE, roped, 0.0).astype(k_ref.dtype)
    odd = jnp.where(lane >= ROPE, pltpu.roll(roped, ROPE, 1), 0.0).astype(k_ref.dtype)
    for hh in range(HEADS):
        k_ref[hh, :, :NOPE] = kn[:, hh * NOPE:(hh + 1) * NOPE]
        k_ref[hh, :, NOPE:] = even if hh % 2 == 0 else odd
        for j in range(nblk):
            vt_ref[hh, j] = vt[hh * VDIM:(hh + 1) * VDIM, j * kvb:(j + 1) * kvb]


def _kv_proj(u_main, u_small, kv_norm_w, w_uk, w_uv_t, cs_tab, sn_tab, b, s):
    t = b * s
    tm = min(s, 512)
    spt = s // tm
    kvb = min(s, KV_BLOCK)
    if tm >= kvb:
        vt_spec = pl.BlockSpec((None, HEADS, tm // kvb, VDIM, kvb), lambda i: (i // spt, 0, i % spt, 0, 0))
    else:
        per = kvb // tm
        vt_spec = pl.BlockSpec((None, HEADS, 1, VDIM, tm),
                               lambda i: (i // spt, 0, (i % spt) // per, 0, (i % spt) % per))
    tab = pl.BlockSpec((tm, LANES), lambda i: (i, 0))
    const = lambda shape: pl.BlockSpec(shape, lambda i: (0, 0))
    return pl.pallas_call(
        _kv_kernel,
        out_shape=(jax.ShapeDtypeStruct((b, HEADS, s, 2 * LANES), BF16),
                   jax.ShapeDtypeStruct((b, HEADS, s // kvb, VDIM, kvb), BF16)),
        grid=(t // tm,),
        in_specs=[pl.BlockSpec((tm, KV_RANK), lambda i: (i, OFF_CKV // KV_RANK)),
                  const((1, KV_RANK)), const((KV_RANK, HEADS * NOPE)), const((HEADS * VDIM, KV_RANK)),
                  tab, tab, tab],
        out_specs=(pl.BlockSpec((None, HEADS, tm, 2 * LANES), lambda i: (i // spt, 0, i % spt, 0)), vt_spec),
        compiler_params=_params("parallel"),
        name="kv_proj",
    )(u_main, kv_norm_w.reshape(1, KV_RANK), w_uk, w_uv_t, u_small, cs_tab, sn_tab)


Q_BLOCK = 256
Q_SUBBLOCKS = 4
KV_UNROLL = 2


def _sublane_all(x, op):
    for shift in (4, 2, 1):
        x = op(x, pltpu.roll(x, shift, 0))
    return x


def _attn_kernel(qn_ref, qr_ref, cs_ref, sn_ref, k_ref, vt_ref, o_ref, qt_sc, s_sc, m_sc, l_sc, acc_sc, *, scale):
    nq = m_sc.shape[0]
    nblk, _, kvb = vt_ref.shape
    kgrp = kvb // SUBLANES
    vgrp = VDIM // SUBLANES
    qr = qr_ref[...].astype(F32)
    roped = qr * cs_ref[...] + _rope_partner(qr) * sn_ref[...]
    q = jnp.concatenate([qn_ref[...].astype(F32), roped], axis=1) * (scale * LOG2_E)
    qt_sc[...] = q.T.astype(BF16)
    m_sc[...] = jnp.full(m_sc.shape, -jnp.inf, F32)
    l_sc[...] = jnp.zeros(l_sc.shape, F32)
    acc_sc[...] = jnp.zeros(acc_sc.shape, F32)

    def scores(c, j):
        kc = k_ref[pl.ds(pl.multiple_of(c * kvb, kvb), kvb), :]
        s_sc[j] = _dot(kc, qt_sc[:, j * Q_BLOCK:(j + 1) * Q_BLOCK])

    for j in range(nq):
        scores(0, j)

    def body(c, carry):
        nxt = jnp.minimum(c + 1, nblk - 1)
        vtc = vt_ref[c]
        for j in range(nq):
            sc = s_sc[j].reshape(kgrp, SUBLANES, Q_BLOCK)
            m_prev = m_sc[j]
            m_new = jnp.maximum(m_prev, _sublane_all(jnp.max(sc, axis=0), jnp.maximum))
            alpha = jnp.exp2(m_prev - m_new)
            p = jnp.exp2(sc - m_new[None])
            l_sc[j] = alpha * l_sc[j] + jnp.sum(p, axis=0)
            pv = _dot(vtc, p.reshape(kvb, Q_BLOCK).astype(BF16))
            scores(nxt, j)
            acc = acc_sc[j].reshape(vgrp, SUBLANES, Q_BLOCK) * alpha[None]
            acc_sc[j] = acc.reshape(VDIM, Q_BLOCK) + pv
            m_sc[j] = m_new
        return carry

    lax.fori_loop(0, nblk, body, 0, unroll=min(KV_UNROLL, nblk))
    for j in range(nq):
        inv = 1.0 / _sublane_all(l_sc[j], jnp.add)
        out_t = (acc_sc[j].reshape(vgrp, SUBLANES, Q_BLOCK) * inv[None]).reshape(VDIM, Q_BLOCK)
        o_ref[j * Q_BLOCK:(j + 1) * Q_BLOCK, :] = out_t.T.astype(o_ref.dtype)


def _attention(u_main, k, vt, cs_tab, sn_tab, b, s):
    t = b * s
    tq = min(s, Q_SUBBLOCKS * Q_BLOCK)
    nq = tq // Q_BLOCK
    qpt = s // tq
    kvb = vt.shape[-1]
    tab = pl.BlockSpec((tq, LANES), lambda bb, h, qi: (bb * qpt + qi, 0))
    kern = functools.partial(_attn_kernel, scale=float(NOPE + ROPE) ** -0.5)
    return pl.pallas_call(
        kern,
        out_shape=jax.ShapeDtypeStruct((t, HEADS * VDIM), BF16),
        grid=(b, HEADS, qpt),
        in_specs=[pl.BlockSpec((tq, NOPE), lambda bb, h, qi: (bb * qpt + qi, h)),
                  pl.BlockSpec((tq, LANES), lambda bb, h, qi: (bb * qpt + qi, OFF_QR // LANES + h // 2)),
                  tab, tab,
                  pl.BlockSpec((None, None, s, 2 * LANES), lambda bb, h, qi: (bb, h, 0, 0)),
                  pl.BlockSpec((None, None, s // kvb, VDIM, kvb), lambda bb, h, qi: (bb, h, 0, 0, 0))],
        out_specs=pl.BlockSpec((tq, VDIM), lambda bb, h, qi: (bb * qpt + qi, h)),
        scratch_shapes=[pltpu.VMEM((2 * LANES, tq), BF16), pltpu.VMEM((nq, kvb, Q_BLOCK), F32),
                        pltpu.VMEM((nq, SUBLANES, Q_BLOCK), F32), pltpu.VMEM((nq, SUBLANES, Q_BLOCK), F32),
                        pltpu.VMEM((nq, VDIM, Q_BLOCK), F32)],
        compiler_params=_params("parallel", "parallel", "arbitrary"),
        name="mla_attention",
    )(u_main, u_main, cs_tab, sn_tab, k, vt)


HALO = BF16_ROWS
CONV_COLS = 512
assert all(o % CONV_COLS == 0 for o in (OFF_XBC, OFF_SCB, OFF_SCC, OFF_SCX))


def _fill_ext(ext_ref, prev, cur, nxt, first, last):
    tm = cur.shape[0]
    ext_ref[:HALO, :] = jnp.where(first, 0.0, prev.astype(F32))
    ext_ref[HALO:HALO + tm, :] = cur.astype(F32)
    ext_ref[HALO + tm:, :] = jnp.where(last, 0.0, nxt.astype(F32))


def _taps(ext_ref, w_ref, ksize, tm):
    pad = ksize // 2
    acc = None
    for kk in range(ksize):
        term = ext_ref[pl.ds(HALO - pad + kk, tm), :] * w_ref[kk:kk + 1, :]
        acc = term if acc is None else acc + term
    return acc


def _ssm_conv_kernel(prev_ref, cur_ref, next_ref, w_ref, b_ref, o_ref, ext_ref, *, tiles_per_seq):
    i = pl.program_id(0)
    tm = cur_ref.shape[0]
    _fill_ext(ext_ref, prev_ref[...], cur_ref[...], next_ref[...],
              i % tiles_per_seq == 0, i % tiles_per_seq == tiles_per_seq - 1)
    o_ref[...] = _silu(_taps(ext_ref, w_ref, SSM_CONV, tm) + b_ref[...]).astype(o_ref.dtype)


def _halo_specs(tm, tc, col0, n_rows):
    r = tm // HALO
    last = n_rows // HALO - 1
    cb = col0 // tc
    return [pl.BlockSpec((HALO, tc), lambda i, j: (jnp.maximum(i * r - 1, 0), cb + j)),
            pl.BlockSpec((tm, tc), lambda i, j: (i, cb + j)),
            pl.BlockSpec((HALO, tc), lambda i, j: (jnp.minimum((i + 1) * r, last), cb + j))]


def _ssm_conv(u_main, conv_w, conv_b, s):
    t = u_main.shape[0]
    tm = min(s, 512)
    tc = CONV_COLS
    kern = functools.partial(_ssm_conv_kernel, tiles_per_seq=s // tm)
    return pl.pallas_call(
        kern,
        out_shape=jax.ShapeDtypeStruct((t, XBC), BF16),
        grid=(t // tm, XBC // tc),
        in_specs=_halo_specs(tm, tc, OFF_XBC, t) + [
            pl.BlockSpec((SSM_CONV, tc), lambda i, j: (0, j)),
            pl.BlockSpec((1, tc), lambda i, j: (0, j))],
        out_specs=pl.BlockSpec((tm, tc), lambda i, j: (i, j)),
        scratch_shapes=[pltpu.VMEM((tm + 2 * HALO, tc), F32)],
        compiler_params=_params("parallel", "parallel"),
        name="ssm_conv",
    )(u_main, u_main, u_main, conv_w, conv_b.reshape(1, XBC))


def _sconv_kernel(cp_ref, cc_ref, cn_ref, xp_ref, xc_ref, xn_ref, b_ref, w_ref, o_ref, ext_ref, *, tiles_per_seq):
    i = pl.program_id(0)
    tm = cc_ref.shape[0]
    f = lambda r: r[...].astype(F32)
    _fill_ext(ext_ref, f(cp_ref) * f(xp_ref), f(cc_ref) * f(xc_ref), f(cn_ref) * f(xn_ref),
              i % tiles_per_seq == 0, i % tiles_per_seq == tiles_per_seq - 1)
    o_ref[...] = (f(b_ref) * _taps(ext_ref, w_ref, SCONV_K, tm)).astype(o_ref.dtype)


def _short_conv(u_main, sconv_w, s):
    t = u_main.shape[0]
    tm = min(s, 512)
    tc = CONV_COLS
    kern = functools.partial(_sconv_kernel, tiles_per_seq=s // tm)
    return pl.pallas_call(
        kern,
        out_shape=jax.ShapeDtypeStruct((t, D), BF16),
        grid=(t // tm, D // tc),
        in_specs=_halo_specs(tm, tc, OFF_SCC, t) + _halo_specs(tm, tc, OFF_SCX, t) + [
            pl.BlockSpec((tm, tc), lambda i, j: (i, OFF_SCB // tc + j)),
            pl.BlockSpec((SCONV_K, tc), lambda i, j: (0, j))],
        out_specs=pl.BlockSpec((tm, tc), lambda i, j: (i, j)),
        scratch_shapes=[pltpu.VMEM((tm + 2 * HALO, tc), F32)],
        compiler_params=_params("parallel", "parallel"),
        name="short_conv",
    )(u_main, u_main, u_main, u_main, u_main, u_main, u_main, sconv_w)


FWD_COL = 2 * SSM_HEADS
BWD_COL = 3 * SSM_HEADS
PAIRS = SSM_HEADS // 2
PAIRS_PER_GROUP = PAIRS // SSM_GROUPS


def _cumsum(x, axis, reverse):
    n = x.shape[axis]
    idx = lax.broadcasted_iota(I32, x.shape, axis)
    k = 1
    while k < n:
        if reverse:
            x = x + jnp.where(idx < n - k, pltpu.roll(x, n - k, axis), 0.0)
        else:
            x = x + jnp.where(idx >= k, pltpu.roll(x, k, axis), 0.0)
        k *= 2
    return x


def _spread_heads(vals, sel, base, exact):
    lane = _lane_iota(vals.shape)
    v = jnp.where(lane >= base, jnp.where(lane < base + SSM_HEADS, vals, 0.0), 0.0)
    hi = v.astype(BF16)
    out = _dot(hi, sel)
    if exact:
        rest = v - hi.astype(F32)
        mid = rest.astype(BF16)
        out = out + _dot(mid, sel) + _dot((rest - mid.astype(F32)).astype(BF16), sel)
    return out


def _ssd_direction(xbc_ref, us_ref, ust_ref, sel, dtb_row, a_row, dtb_col, a_col, d_row, y_ref, st_ref, *, reverse):
    base = BWD_COL if reverse else FWD_COL
    edge = 0 if reverse else CHUNK - 1
    dt_c = _softplus(us_ref[...] + dtb_row)
    cum_c = _cumsum(dt_c * a_row, 0, reverse)
    dt_r = _softplus(ust_ref[...] + dtb_col)
    cum_r = _cumsum(dt_r * a_col, 1, reverse)
    tot = cum_c[edge:edge + 1, :]
    dt_x = _spread_heads(dt_c, sel, base, False)
    ecum_x = _spread_heads(jnp.exp(cum_c), sel, base, False)
    dec_x = _spread_heads(jnp.exp(tot - cum_c) * dt_c, sel, base, False)
    etot_x = _spread_heads(jnp.broadcast_to(jnp.exp(tot), (SUBLANES, LANES)), sel, base, True)[0:1, :]

    li = lax.broadcasted_iota(I32, (CHUNK, CHUNK), 0)
    si = lax.broadcasted_iota(I32, (CHUNK, CHUNK), 1)
    mask = (li <= si) if reverse else (li >= si)
    lane = _lane_iota((CHUNK, LANES))
    lo = lane < SSM_P

    for g in range(SSM_GROUPS):
        bg = xbc_ref[:, SSM_INNER + g * SSM_N:SSM_INNER + (g + 1) * SSM_N]
        cg = xbc_ref[:, SSM_INNER + (SSM_GROUPS + g) * SSM_N:SSM_INNER + (SSM_GROUPS + g + 1) * SSM_N]
        cb = _dot_nt(cg, bg)
        for j in range(PAIRS_PER_GROUP):
            pr = g * PAIRS_PER_GROUP + j
            c1, c2 = base + 2 * pr, base + 2 * pr + 1
            cols = slice(pr * LANES, (pr + 1) * LANES)

            xs = xbc_ref[:, cols].astype(F32)
            xdt = xs * dt_x[:, cols]
            scores = []
            for c in (c1, c2):
                seg = cum_c[:, c:c + 1] - cum_r[c:c + 1, :]
                scores.append((cb * jnp.exp(jnp.where(mask, seg, -jnp.inf))).astype(BF16))
            xb = xdt.astype(BF16)
            zero = jnp.zeros_like(xb)
            x_bd = jnp.concatenate([jnp.where(lo, xb, zero), jnp.where(lo, zero, xb)], axis=0)
            y = _dot(jnp.concatenate(scores, axis=1), x_bd)
            st = st_ref[g, :, j * LANES:(j + 1) * LANES]
            y = y + _dot(cg, st.astype(BF16)) * ecum_x[:, cols]
            if not reverse:
                y = y + xs * d_row[:, cols]
            y_ref[:, cols] = y.astype(y_ref.dtype)
            new = _dot_tn(bg, (xs * dec_x[:, cols]).astype(BF16))
            st_ref[g, :, j * LANES:(j + 1) * LANES] = st * etot_x[:, cols] + new


def _ssd_kernel(xf_ref, usf_ref, ustf_ref, xb_ref, usb_ref, ustb_ref, sel_ref, dtb_row_ref, alog_row_ref,
                dtb_col_ref, alog_col_ref, d_ref, yf_ref, yb_ref, stf_ref, stb_ref):
    @pl.when(pl.program_id(1) == 0)
    def _():
        stf_ref[...] = jnp.zeros(stf_ref.shape, F32)
        stb_ref[...] = jnp.zeros(stb_ref.shape, F32)

    dtb_row, dtb_col = dtb_row_ref[...], dtb_col_ref[...]
    a_row, a_col = -jnp.exp(alog_row_ref[...]), -jnp.exp(alog_col_ref[...])
    d_row = d_ref[...]
    sel = sel_ref[...]
    _ssd_direction(xf_ref, usf_ref, ustf_ref, sel, dtb_row, a_row, dtb_col, a_col, d_row, yf_ref, stf_ref,
                   reverse=False)
    _ssd_direction(xb_ref, usb_ref, ustb_ref, sel, dtb_row, a_row, dtb_col, a_col, d_row, yb_ref, stb_ref,
                   reverse=True)


def _ssd_scan(xbc_conv, u_small, u_small_t, a_log, dt_bias, d_skip, b, s):
    t = b * s
    nc = s // CHUNK
    pad = jnp.zeros((2 * SSM_HEADS,), F32)
    dtb = jnp.concatenate([pad, dt_bias.reshape(-1)])
    alog = jnp.concatenate([pad, a_log.reshape(-1)])
    d_row = jnp.repeat(d_skip, SSM_P)[None, :]
    head_of_lane = np.concatenate([np.full(2 * SSM_HEADS, -1), np.arange(SSM_HEADS), np.arange(SSM_HEADS)])
    sel = jnp.asarray(head_of_lane[:, None] == np.repeat(np.arange(SSM_HEADS), SSM_P)[None, :], BF16)
    fw = lambda bb, i: bb * nc + i
    bw = lambda bb, i: bb * nc + nc - 1 - i
    row = lambda w: pl.BlockSpec((1, w), lambda bb, i: (0, 0))
    col = pl.BlockSpec((LANES, 1), lambda bb, i: (0, 0))

    def chunk_specs(idx):
        return [pl.BlockSpec((CHUNK, XBC), lambda bb, i: (idx(bb, i), 0)),
                pl.BlockSpec((CHUNK, LANES), lambda bb, i: (idx(bb, i), 0)),
                pl.BlockSpec((LANES, CHUNK), lambda bb, i: (0, idx(bb, i)))]

    return pl.pallas_call(
        _ssd_kernel,
        out_shape=(jax.ShapeDtypeStruct((t, SSM_INNER), BF16),) * 2,
        grid=(b, nc),
        in_specs=chunk_specs(fw) + chunk_specs(bw) + [
            pl.BlockSpec((LANES, SSM_INNER), lambda bb, i: (0, 0)),
            row(LANES), row(LANES), col, col, row(SSM_INNER)],
        out_specs=(pl.BlockSpec((CHUNK, SSM_INNER), lambda bb, i: (fw(bb, i), 0)),
                   pl.BlockSpec((CHUNK, SSM_INNER), lambda bb, i: (bw(bb, i), 0))),
        scratch_shapes=[pltpu.VMEM((SSM_GROUPS, SSM_N, SSM_INNER // SSM_GROUPS), F32)] * 2,
        compiler_params=_params("parallel", "arbitrary"),
        name="ssd_scan",
    )(xbc_conv, u_small, u_small_t, xbc_conv, u_small, u_small_t, sel,
      dtb[None, :], alog[None, :], dtb[:, None], alog[:, None], d_row)


def _ssm_post_kernel(yf_ref, yb_ref, z_ref, w_ref, o_ref):
    y = (yf_ref[...].astype(F32) + yb_ref[...].astype(F32)) * _silu(z_ref[...].astype(F32))
    o_ref[...] = _rms(y, w_ref[...]).astype(o_ref.dtype)


def _ssm_post(y_fwd, y_bwd, u_main, norm_w):
    t = y_fwd.shape[0]
    tm = min(t, 512)
    gw = SSM_INNER // SSM_GROUPS
    blk = pl.BlockSpec((tm, gw), lambda i, g: (i, g))
    return pl.pallas_call(
        _ssm_post_kernel,
        out_shape=jax.ShapeDtypeStruct((t, SSM_INNER), BF16),
        grid=(t // tm, SSM_GROUPS),
        in_specs=[blk, blk, pl.BlockSpec((tm, gw), lambda i, g: (i, OFF_Z // gw + g)),
                  pl.BlockSpec((1, gw), lambda i, g: (0, g))],
        out_specs=blk,
        compiler_params=_params("parallel", "parallel"),
        name="ssm_gate_norm",
    )(y_fwd, y_bwd, u_main, norm_w.reshape(1, SSM_INNER))


def _merge_kernel(ya_ref, ys_ref, yc_ref, pa_ref, ps_ref, pc_ref, ga_ref, gs_ref, gc_ref, o_ref):
    g = lambda r: _sigmoid(r[...].astype(F32))
    m = (g(ga_ref) * _dot(ya_ref[...], pa_ref[...]) + g(gs_ref) * _dot(ys_ref[...], ps_ref[...])
         + g(gc_ref) * _dot(yc_ref[...], pc_ref[...]))
    o_ref[...] = m.astype(o_ref.dtype)


def _merge(y_att, y_ssm, y_conv, w_branch, u_main):
    t = y_att.shape[0]
    tm = min(t, 512)
    tn = 512
    yb = pl.BlockSpec((tm, D), lambda i, j: (i, 0))
    pspec = lambda r: pl.BlockSpec((D, tn), lambda i, j: (r, j))
    gspec = lambda r: pl.BlockSpec((tm, tn), lambda i, j: (i, OFF_GATE // tn + r * (D // tn) + j))
    return pl.pallas_call(
        _merge_kernel,
        out_shape=jax.ShapeDtypeStruct((t, D), BF16),
        grid=(t // tm, D // tn),
        in_specs=[yb, yb, yb, pspec(0), pspec(1), pspec(2), gspec(0), gspec(1), gspec(2)],
        out_specs=pl.BlockSpec((tm, tn), lambda i, j: (i, j)),
        compiler_params=_params("parallel", "arbitrary"),
        name="branch_merge",
    )(y_att, y_ssm, y_conv, w_branch, w_branch, w_branch, u_main, u_main, u_main)


def _split_bf16(x):
    hi = x.astype(BF16)
    lo = (x - hi.astype(F32)).astype(BF16)
    return hi, lo


def _wo_kernel(m_ref, h_ref, wo_ref, nw_ref, rhi_ref, rlo_ref, rb_ref, h1_ref, t_ref, lg_ref):
    h1 = h_ref[...] + _dot(m_ref[...], wo_ref[...])
    h1_ref[...] = h1
    tn = _rms(h1, nw_ref[...])
    for sl in range(SLAB):
        t_ref[pl.ds(sl, tn.shape[0], stride=SLAB), :] = tn[:, sl * LANES:(sl + 1) * LANES]
    hi, lo = _split_bf16(tn)
    lg_ref[...] = _dot(hi, rhi_ref[...]) + _dot(lo, rhi_ref[...]) + _dot(hi, rlo_ref[...]) + rb_ref[...]


def _out_proj(merged, h, w_o, ffn_norm_w, r_hi, r_lo, r_bias):
    t = h.shape[0]
    tm = min(t, 256)
    blk = pl.BlockSpec((tm, D), lambda i: (i, 0))
    const = lambda shape: pl.BlockSpec(shape, lambda i: (0, 0))
    lg = pl.BlockSpec((tm, LANES), lambda i: (i, 0))
    return pl.pallas_call(
        _wo_kernel,
        out_shape=(jax.ShapeDtypeStruct((t, D), F32), jax.ShapeDtypeStruct((t * SLAB, LANES), F32),
                   jax.ShapeDtypeStruct((t, LANES), F32)),
        grid=(t // tm,),
        in_specs=[blk, blk, const((D, D)), const((1, D)), const((D, LANES)), const((D, LANES)),
                  const((1, LANES))],
        out_specs=(blk, pl.BlockSpec((tm * SLAB, LANES), lambda i: (i, 0)), lg),
        compiler_params=_params("parallel"),
        name="out_proj_router",
    )(merged, h, w_o, ffn_norm_w.reshape(1, D), r_hi, r_lo, r_bias)


def _route_kernel(lg_ref, id_ref, wt_ref, rk_ref, cnt_ref, carry_sc):
    @pl.when(pl.program_id(0) == 0)
    def _():
        carry_sc[...] = jnp.zeros(carry_sc.shape, F32)

    lg = lg_ref[...]
    lane = _lane_iota(lg.shape).astype(F32)
    big = float(LANES)
    ninf = -jnp.inf
    first = lambda hit: jnp.min(jnp.where(hit, lane, big), axis=1, keepdims=True)

    gl = jnp.where(lane < N_GROUPS, lg, ninf)
    gmax = jnp.max(gl, axis=1, keepdims=True)
    gidx = first(gl == gmax)
    g_w = 1.0 / jnp.sum(jnp.exp(gl - gmax), axis=1, keepdims=True)

    in_group = (lane >= N_GROUPS + gidx * EPG) & (lane < N_GROUPS + (gidx + 1.0) * EPG)
    el = jnp.where(in_group, lg, ninf)
    v0 = jnp.max(el, axis=1, keepdims=True)
    i0 = first(el == v0)
    el = jnp.where(lane == i0, ninf, el)
    v1 = jnp.max(el, axis=1, keepdims=True)
    i1 = first(el == v1)
    e1 = jnp.exp(v1 - v0)
    w0 = g_w / (1.0 + e1)
    w1 = g_w * e1 / (1.0 + e1)
    id_ref[...] = jnp.where(lane == 0.0, i0, i1).astype(I32) - N_GROUPS
    wt_ref[...] = jnp.where(lane == 0.0, w0, w1)

    hit0 = lane == i0 - N_GROUPS
    hit1 = lane == i1 - N_GROUPS
    onehot = jnp.where(hit0, 1.0, jnp.where(hit1, 1.0, 0.0))
    tm = lg.shape[0]
    earlier = lax.broadcasted_iota(I32, (tm, tm), 1) < lax.broadcasted_iota(I32, (tm, tm), 0)
    carry = carry_sc[0:1, :]
    before = _dot(jnp.where(earlier, 1.0, 0.0).astype(BF16), onehot.astype(BF16)) + carry
    r0 = jnp.sum(jnp.where(hit0, before, 0.0), axis=1, keepdims=True)
    r1 = jnp.sum(jnp.where(hit1, before, 0.0), axis=1, keepdims=True)
    rk_ref[...] = jnp.where(lane == 0.0, r0, r1).astype(I32)
    carry_sc[...] = jnp.broadcast_to(carry + jnp.sum(onehot, axis=0, keepdims=True), carry_sc.shape)
    cnt_ref[...] = carry_sc[...]


def _route(logits):
    t = logits.shape[0]
    tm = min(t, 512)
    blk = pl.BlockSpec((tm, LANES), lambda i: (i, 0))
    cnt = pl.BlockSpec((SUBLANES, LANES), lambda i: (0, 0))
    return pl.pallas_call(
        _route_kernel,
        out_shape=(jax.ShapeDtypeStruct((t, LANES), I32), jax.ShapeDtypeStruct((t, LANES), F32),
                   jax.ShapeDtypeStruct((t, LANES), I32), jax.ShapeDtypeStruct((SUBLANES, LANES), F32)),
        grid=(t // tm,),
        in_specs=[blk],
        out_specs=(blk, blk, blk, cnt),
        scratch_shapes=[pltpu.VMEM((SUBLANES, LANES), F32)],
        compiler_params=_params("arbitrary"),
        name="route",
    )(logits)


def _dispatch_plan(ids, wts, ranks, counts, t, tile):
    n_rows = 2 * t
    n_tiles = n_rows // tile + N_EXPERTS
    n_pos = n_tiles * tile
    e = ids[:, :2].reshape(-1)
    w = wts[:, :2].reshape(-1)
    rank = ranks[:, :2].reshape(-1)
    counts = counts[0, :N_EXPERTS].astype(I32)
    padded = ((counts + tile - 1) // tile) * tile
    ends = jnp.cumsum(padded)
    starts = ends - padded
    pos = starts[e] + rank
    r = jnp.arange(n_rows, dtype=I32)
    packed = jnp.stack([r, lax.bitcast_convert_type(w, I32)], axis=1)
    slots = jnp.full((n_pos, 2), -1, I32).at[pos].set(packed)
    src = slots[:, 0]
    valid = src >= 0
    tile_start = jnp.arange(n_tiles, dtype=I32) * tile
    tile_e = jnp.minimum(jnp.sum(tile_start[:, None] >= ends[None, :], axis=1), N_EXPERTS - 1).astype(I32)
    in_tile = jnp.arange(n_pos, dtype=I32) % tile
    row_tok = jnp.where(valid, src // 2, 0)
    row_dst = jnp.where(valid, (src % 2) * t + src // 2, n_rows + in_tile)
    row_w = jnp.where(valid, lax.bitcast_convert_type(slots[:, 1], F32), 0.0)
    n_used = (ends[-1] // tile).astype(I32).reshape(1)
    return (row_tok.reshape(n_tiles, 1, tile), row_dst.reshape(n_tiles, 1, tile),
            row_w.reshape(n_pos, 1), tile_e, n_used)


DMA_UNROLL = 8


def _moe_kernel(te_ref, nu_ref, tok_ref, tokn_ref, dstp_ref, dst_ref, rw_ref, t_hbm, wg_ref, wu_ref, wd_ref, o_hbm,
                xbuf, ybuf, wg_sc, wu_sc, wd_sc, gsem, ssem):
    i = pl.program_id(0)
    n_used = nu_ref[0]
    tile = rw_ref.shape[0]
    rows = tile * PITCH
    slot = i % 2

    def gather_copy(tok, r, slot_):
        dst = xbuf.at[pl.ds(pl.multiple_of(slot_ * rows + r * PITCH, SUBLANES), SLAB), :]
        return pltpu.make_async_copy(t_hbm.at[tok[0, 0, r]], dst, gsem.at[slot_])

    def scatter_copy(dst, r):
        src = ybuf.at[pl.ds(pl.multiple_of(r * PITCH, SUBLANES), SLAB), :]
        return pltpu.make_async_copy(src, o_hbm.at[dst[0, 0, r]], ssem)

    def wait_rows(buf, sem):
        done = buf.at[pl.ds(0, tile * SLAB), :]
        pltpu.make_async_copy(done, done, sem).wait()

    def looped(make):
        def body(r, c):
            make(r).start()
            return c
        lax.fori_loop(0, tile, body, 0, unroll=DMA_UNROLL)

    @pl.when(i == 0)
    def _():
        ybuf[...] = jnp.zeros(ybuf.shape, F32)
        looped(lambda r: gather_copy(tok_ref, r, 0))

    @pl.when(i < n_used)
    def _():
        @pl.when((i == 0) | (te_ref[i] != te_ref[jnp.maximum(i - 1, 0)]))
        def _():
            wg_sc[...] = wg_ref[...].astype(BF16)
            wu_sc[...] = wu_ref[...].astype(BF16)
            wd_sc[...] = wd_ref[...].astype(BF16)

        @pl.when(i + 1 < n_used)
        def _():
            looped(lambda r: gather_copy(tokn_ref, r, 1 - slot))

        wait_rows(xbuf, gsem.at[slot])
        base = slot * rows
        x = jnp.concatenate([xbuf[pl.ds(base + sl, tile, stride=PITCH), :] for sl in range(SLAB)],
                            axis=1).astype(BF16)
        for r in range(tile):
            scatter_copy(dstp_ref, r).start()
        gate = _dot(x, wg_sc[...])
        mid = (_silu(gate) * _dot(x, wu_sc[...])).astype(BF16)
        y = _dot(mid, wd_sc[...]) * rw_ref[...]
        wait_rows(ybuf, ssem)
        for sl in range(SLAB):
            ybuf[pl.ds(sl, tile, stride=PITCH), :] = y[:, sl * LANES:(sl + 1) * LANES]

        @pl.when(i == n_used - 1)
        def _():
            looped(lambda r: scatter_copy(dst_ref, r))
            wait_rows(ybuf, ssem)


def _moe(t_norm, plan, layer, w_gate, w_up, w_down):
    t = t_norm.shape[0] // SLAB
    row_tok, row_dst, row_w, tile_e, n_used = plan
    n_tiles, _, tile = row_tok.shape
    t_slab = t_norm.reshape(t, SLAB, LANES)
    extra = (2 * t + jnp.arange(tile, dtype=I32)).reshape(1, 1, tile)
    dst_prev = jnp.concatenate([extra, row_dst[:-1]], axis=0)
    smem = lambda f: pl.BlockSpec((1, 1, tile), lambda i, te, nu: (f(i), 0, 0), memory_space=pltpu.SMEM)
    wspec = lambda a, b: pl.BlockSpec((None, None, a, b), lambda i, te, nu: (layer, te[i], 0, 0))
    out = pl.pallas_call(
        _moe_kernel,
        out_shape=jax.ShapeDtypeStruct((2 * t + tile, SLAB, LANES), F32),
        grid_spec=pltpu.PrefetchScalarGridSpec(
            num_scalar_prefetch=2,
            grid=(n_tiles,),
            in_specs=[smem(lambda i: i), smem(lambda i: jnp.minimum(i + 1, n_tiles - 1)),
                      smem(lambda i: i), smem(lambda i: i),
                      pl.BlockSpec((tile, 1), lambda i, te, nu: (i, 0)),
                      pl.BlockSpec(memory_space=pl.ANY),
                      wspec(D, D_EXPERT), wspec(D, D_EXPERT), wspec(D_EXPERT, D)],
            out_specs=pl.BlockSpec(memory_space=pl.ANY),
            scratch_shapes=[pltpu.VMEM((2 * tile * PITCH, LANES), F32), pltpu.VMEM((tile * PITCH, LANES), F32),
                            pltpu.VMEM((D, D_EXPERT), BF16), pltpu.VMEM((D, D_EXPERT), BF16),
                            pltpu.VMEM((D_EXPERT, D), BF16),
                            pltpu.SemaphoreType.DMA((2,)), pltpu.SemaphoreType.DMA(())]),
        compiler_params=_params("arbitrary"),
        name="moe_experts",
    )(tile_e, n_used, row_tok, row_tok, dst_prev, row_dst, row_w, t_slab, w_gate, w_up, w_down)
    return out.reshape((2 * t + tile) * SLAB, LANES)


def kernel(x, positions, attn_norm_w, w_in, kv_norm_w, w_uk, w_uv, ssm_conv_w, ssm_conv_b, ssm_A_log, ssm_dt_bias, ssm_D, ssm_norm_w, sconv_w, w_branch, w_o, ffn_norm_w, router_group_w, router_group_b, router_expert_w, router_expert_b, expert_w_gate, expert_w_up, expert_w_down, final_norm_w):
    b, s, _ = x.shape
    t = b * s
    depth = w_in.shape[0]
    h = x.reshape(t, D)
    cs_tab, sn_tab = _rope_tables(positions)
    moe_out = None
    for l in range(depth):
        if moe_out is None:
            n = _norm(h, attn_norm_w[l], BF16)
        else:
            h, n = _add_norm(h, moe_out, attn_norm_w[l], BF16)
        w_perm = _permute_in_proj(w_in[l])
        w_small = w_perm[:, OFF_SMALL:OFF_SMALL + LANES]
        u_main, u_small, u_small_t = _in_proj(n, w_perm, w_small, w_small.T)

        k, vt = _kv_proj(u_main, u_small, kv_norm_w[l], w_uk[l].astype(BF16), w_uv[l].T.astype(BF16),
                         cs_tab, sn_tab, b, s)
        y_att = _attention(u_main, k, vt, cs_tab, sn_tab, b, s)

        xbc_conv = _ssm_conv(u_main, ssm_conv_w[l], ssm_conv_b[l], s)
        y_fwd, y_bwd = _ssd_scan(xbc_conv, u_small, u_small_t, ssm_A_log[l], ssm_dt_bias[l], ssm_D[l], b, s)
        y_ssm = _ssm_post(y_fwd, y_bwd, u_main, ssm_norm_w[l])

        y_conv = _short_conv(u_main, sconv_w[l], s)

        merged = _merge(y_att, y_ssm, y_conv, w_branch[l].astype(BF16), u_main)

        r_w = jnp.concatenate([router_group_w[l], router_expert_w[l],
                               jnp.zeros((D, LANES - N_GROUPS - N_EXPERTS), F32)], axis=1)
        r_b = jnp.concatenate([router_group_b[l], router_expert_b[l],
                               jnp.zeros((LANES - N_GROUPS - N_EXPERTS,), F32)])[None, :]
        r_hi, r_lo = _split_bf16(r_w)
        h, t_norm, logits = _out_proj(merged, h, w_o[l].astype(BF16), ffn_norm_w[l], r_hi, r_lo, r_b)

        ids, wts, ranks, counts = _route(logits)
        plan = _dispatch_plan(ids, wts, ranks, counts, t, MOE_TILE)
        moe_out = _moe(t_norm, plan, l, expert_w_gate, expert_w_up, expert_w_down)

    h, out = _add_norm(h, moe_out, final_norm_w, F32)
    return out.reshape(b, s, D)
```

```python
import functools

import numpy as np
import jax
import jax.numpy as jnp
from jax import lax
from jax.experimental import pallas as pl
from jax.experimental.pallas import tpu as pltpu

F32 = jnp.float32
BF16 = jnp.bfloat16
I32 = jnp.int32

D = 2048
EPS = 1e-6
LOG2_E = 1.4426950408889634
HEADS = 16
NOPE = 128
ROPE = 64
VDIM = 128
KV_RANK = 512
ROPE_THETA = 10000.0
SSM_HEADS = 32
SSM_P = 64
SSM_INNER = SSM_HEADS * SSM_P
SSM_GROUPS = 4
SSM_N = 128
SSM_CONV = 5
CHUNK = 128
XBC = SSM_INNER + 2 * SSM_GROUPS * SSM_N
SCONV_K = 3
N_GROUPS = 8
EPG = 4
N_EXPERTS = N_GROUPS * EPG
D_EXPERT = 512
Q_DIM = HEADS * (NOPE + ROPE)
IN_COLS = Q_DIM + KV_RANK + ROPE + SSM_INNER + XBC + 2 * SSM_HEADS + 3 * D + 3 * D

LANES = 128
SUBLANES = 8
BF16_ROWS = 16
V7X_VMEM_BYTES = 64 * 1024 * 1024
VMEM_LIMIT = 56 * 1024 * 1024

OFF_QN = 0
OFF_QR = OFF_QN + HEADS * NOPE
OFF_CKV = OFF_QR + HEADS * ROPE
OFF_Z = OFF_CKV + KV_RANK
OFF_XBC = OFF_Z + SSM_INNER
OFF_SCB = OFF_XBC + XBC
OFF_SCC = OFF_SCB + D
OFF_SCX = OFF_SCC + D
OFF_GATE = OFF_SCX + D
OFF_SMALL = OFF_GATE + 3 * D
assert OFF_SMALL + LANES == IN_COLS
IN_PROJ_TN = 1024
U_COLS = -(-IN_COLS // IN_PROJ_TN) * IN_PROJ_TN

SLAB = D // LANES
PITCH = 24
MOE_TILE = 384


def _permute_in_proj(w):
    o_ckv = Q_DIM
    o_kr = o_ckv + KV_RANK
    o_z = o_kr + ROPE
    o_dt = o_z + SSM_INNER + XBC
    o_rest = o_dt + 2 * SSM_HEADS
    q = w[:, :Q_DIM].reshape(D, HEADS, NOPE + ROPE)
    parts = [q[:, :, :NOPE].reshape(D, HEADS * NOPE), q[:, :, NOPE:].reshape(D, HEADS * ROPE),
             w[:, o_ckv:o_kr], w[:, o_z:o_dt], w[:, o_rest:], w[:, o_kr:o_z], w[:, o_dt:o_rest]]
    parts.append(jnp.zeros((D, U_COLS - IN_COLS), w.dtype))
    out = jnp.concatenate([p.astype(BF16) for p in parts], axis=1)
    assert out.shape == (D, U_COLS)
    return out


def _params(*sem):
    return pltpu.CompilerParams(dimension_semantics=tuple(sem), vmem_limit_bytes=VMEM_LIMIT)


def _dot(a, b):
    return jnp.dot(a, b, preferred_element_type=F32)


def _dot_nt(a, b):
    return lax.dot_general(a, b, (((1,), (1,)), ((), ())), preferred_element_type=F32)


def _dot_tn(a, b):
    return lax.dot_general(a, b, (((0,), (0,)), ((), ())), preferred_element_type=F32)


def _sigmoid(x):
    return 1.0 / (1.0 + jnp.exp(-x))


def _silu(x):
    return x * _sigmoid(x)


def _softplus(x):
    return jnp.maximum(x, 0.0) + jnp.log(1.0 + jnp.exp(-jnp.abs(x)))


def _lane_iota(shape):
    return lax.broadcasted_iota(I32, shape, len(shape) - 1)


def _rope_partner(v):
    lane = _lane_iota(v.shape)
    return jnp.where((lane & 32) == 0, pltpu.roll(v, LANES - 32, 1), pltpu.roll(v, 32, 1))


def _rope_tab_kernel(pos_ref, freq_ref, sign_ref, cs_ref, sn_ref):
    ang = pos_ref[...].astype(F32) * freq_ref[...]
    cs_ref[...] = jnp.cos(ang)
    sn_ref[...] = jnp.sin(ang) * sign_ref[...]


def _rope_tables(positions):
    t = positions.size
    tm = min(t, 1024)
    freqs = ROPE_THETA ** (-jnp.arange(0, ROPE, 2, dtype=F32) / ROPE)
    freq_row = jnp.tile(freqs, 4)[None, :]
    sign_row = jnp.tile(jnp.concatenate([-jnp.ones(32, F32), jnp.ones(32, F32)]), 2)[None, :]
    row = pl.BlockSpec((1, LANES), lambda i: (0, 0))
    tab = pl.BlockSpec((tm, LANES), lambda i: (i, 0))
    return pl.pallas_call(
        _rope_tab_kernel,
        out_shape=(jax.ShapeDtypeStruct((t, LANES), F32),) * 2,
        grid=(t // tm,),
        in_specs=[pl.BlockSpec((tm, 1), lambda i: (i, 0)), row, row],
        out_specs=(tab, tab),
        compiler_params=_params("parallel"),
        name="rope_tables",
    )(positions.reshape(t, 1), freq_row, sign_row)


def _rms(x, w):
    return x * lax.rsqrt(jnp.mean(x * x, axis=-1, keepdims=True) + EPS) * w


def _norm_kernel(h_ref, w_ref, n_ref):
    n_ref[...] = _rms(h_ref[...], w_ref[...]).astype(n_ref.dtype)


def _rows_from_slabs(ref, tm):
    return jnp.concatenate([ref[pl.ds(sl, tm, stride=SLAB), :] for sl in range(SLAB)], axis=1)


def _add_norm_kernel(h_ref, o0_ref, o1_ref, w_ref, hs_ref, n_ref):
    tm = h_ref.shape[0]
    hs = h_ref[...] + _rows_from_slabs(o0_ref, tm) + _rows_from_slabs(o1_ref, tm)
    hs_ref[...] = hs
    n_ref[...] = _rms(hs, w_ref[...]).astype(n_ref.dtype)


def _norm(h, w, out_dtype):
    t = h.shape[0]
    tm = min(t, 256)
    blk = pl.BlockSpec((tm, D), lambda i: (i, 0))
    return pl.pallas_call(
        _norm_kernel,
        out_shape=jax.ShapeDtypeStruct((t, D), out_dtype),
        grid=(t // tm,),
        in_specs=[blk, pl.BlockSpec((1, D), lambda i: (0, 0))],
        out_specs=blk,
        compiler_params=_params("parallel"),
        name="rmsnorm",
    )(h, w.reshape(1, D))


def _add_norm(h, o, w, out_dtype):
    t = h.shape[0]
    tm = min(t, 256)
    nt = t // tm
    blk = pl.BlockSpec((tm, D), lambda i: (i, 0))
    return pl.pallas_call(
        _add_norm_kernel,
        out_shape=(jax.ShapeDtypeStruct((t, D), F32), jax.ShapeDtypeStruct((t, D), out_dtype)),
        grid=(nt,),
        in_specs=[blk, pl.BlockSpec((tm * SLAB, LANES), lambda i: (i, 0)),
                  pl.BlockSpec((tm * SLAB, LANES), lambda i: (nt + i, 0)),
                  pl.BlockSpec((1, D), lambda i: (0, 0))],
        out_specs=(blk, blk),
        compiler_params=_params("parallel"),
        name="add_rmsnorm",
    )(h, o, o, w.reshape(1, D))


def _inproj_kernel(n_ref, wm_ref, ws_ref, wst_ref, um_ref, us_ref, ust_ref):
    x = n_ref[...]
    um_ref[...] = _dot(x, wm_ref[...]).astype(um_ref.dtype)

    @pl.when(pl.program_id(1) == 0)
    def _():
        us_ref[...] = _dot(x, ws_ref[...])
        ust_ref[...] = _dot_nt(wst_ref[...], x)


def _in_proj(n, w_main, w_small, w_small_t):
    t = n.shape[0]
    tm = min(t, 2048)
    tn = IN_PROJ_TN
    return pl.pallas_call(
        _inproj_kernel,
        out_shape=(jax.ShapeDtypeStruct((t, U_COLS), BF16),
                   jax.ShapeDtypeStruct((t, LANES), F32),
                   jax.ShapeDtypeStruct((LANES, t), F32)),
        grid=(t // tm, U_COLS // tn),
        in_specs=[pl.BlockSpec((tm, D), lambda i, j: (i, 0)),
                  pl.BlockSpec((D, tn), lambda i, j: (0, j)),
                  pl.BlockSpec((D, LANES), lambda i, j: (0, 0)),
                  pl.BlockSpec((LANES, D), lambda i, j: (0, 0))],
        out_specs=(pl.BlockSpec((tm, tn), lambda i, j: (i, j)),
                   pl.BlockSpec((tm, LANES), lambda i, j: (i, 0)),
                   pl.BlockSpec((LANES, tm), lambda i, j: (0, i))),
        compiler_params=_params("parallel", "arbitrary"),
        name="in_proj",
    )(n, w_main, w_small, w_small_t)


KV_BLOCK = 1024


def _kv_kernel(ckv_ref, nw_ref, wuk_ref, wuvt_ref, us_ref, cs_ref, sn_ref, k_ref, vt_ref):
    nblk, kvb = vt_ref.shape[1], vt_ref.shape[3]
    c = _rms(ckv_ref[...].astype(F32), nw_ref[...]).astype(BF16)
    kn = _dot(c, wuk_ref[...]).astype(k_ref.dtype)
    vt = _dot_nt(wuvt_ref[...], c).astype(vt_ref.dtype)
    u = us_ref[...]
    roped = u * cs_ref[...] + _rope_partner(u) * sn_ref[...]
    lane = _lane_iota(roped.shape)
    even = jnp.where(lane < ROPE, roped, 0.0).astype(k_ref.dtype)
    odd = jnp.where(lane >= ROPE, pltpu.roll(roped, ROPE, 1), 0.0).astype(k_ref.dtype)
    for hh in range(HEADS):
        k_ref[hh, :, :NOPE] = kn[:, hh * NOPE:(hh + 1) * NOPE]
        k_ref[hh, :, NOPE:] = even if hh % 2 == 0 else odd
        for j in range(nblk):
            vt_ref[hh, j] = vt[hh * VDIM:(hh + 1) * VDIM, j * kvb:(j + 1) * kvb]


def _kv_proj(u_main, u_small, kv_norm_w, w_uk, w_uv_t, cs_tab, sn_tab, b, s):
    t = b * s
    tm = min(s, 512)
    spt = s // tm
    kvb = min(s, KV_BLOCK)
    if tm >= kvb:
        vt_spec = pl.BlockSpec((None, HEADS, tm // kvb, VDIM, kvb), lambda i: (i // spt, 0, i % spt, 0, 0))
    else:
        per = kvb // tm
        vt_spec = pl.BlockSpec((None, HEADS, 1, VDIM, tm),
                               lambda i: (i // spt, 0, (i % spt) // per, 0, (i % spt) % per))
    tab = pl.BlockSpec((tm, LANES), lambda i: (i, 0))
    const = lambda shape: pl.BlockSpec(shape, lambda i: (0, 0))
    return pl.pallas_call(
        _kv_kernel,
        out_shape=(jax.ShapeDtypeStruct((b, HEADS, s, 2 * LANES), BF16),
                   jax.ShapeDtypeStruct((b, HEADS, s // kvb, VDIM, kvb), BF16)),
        grid=(t // tm,),
        in_specs=[pl.BlockSpec((tm, KV_RANK), lambda i: (i, OFF_CKV // KV_RANK)),
                  const((1, KV_RANK)), const((KV_RANK, HEADS * NOPE)), const((HEADS * VDIM, KV_RANK)),
                  tab, tab, tab],
        out_specs=(pl.BlockSpec((None, HEADS, tm, 2 * LANES), lambda i: (i // spt, 0, i % spt, 0)), vt_spec),
        compiler_params=_params("parallel"),
        name="kv_proj",
    )(u_main, kv_norm_w.reshape(1, KV_RANK), w_uk, w_uv_t, u_small, cs_tab, sn_tab)


Q_BLOCK = 256
Q_SUBBLOCKS = 4
KV_UNROLL = 1


def _sublane_all(x, op):
    for shift in (4, 2, 1):
        x = op(x, pltpu.roll(x, shift, 0))
    return x


def _attn_kernel(qn_ref, qr_ref, cs_ref, sn_ref, k_ref, vt_ref, o_ref, qt_sc, s_sc, m_sc, l_sc, acc_sc, *, scale):
    nq = m_sc.shape[0]
    nblk, _, kvb = vt_ref.shape
    kgrp = kvb // SUBLANES
    vgrp = VDIM // SUBLANES
    qr = qr_ref[...].astype(F32)
    roped = qr * cs_ref[...] + _rope_partner(qr) * sn_ref[...]
    q = jnp.concatenate([qn_ref[...].astype(F32), roped], axis=1) * (scale * LOG2_E)
    qt_sc[...] = q.T.astype(BF16)
    m_sc[...] = jnp.full(m_sc.shape, -jnp.inf, F32)
    l_sc[...] = jnp.zeros(l_sc.shape, F32)
    acc_sc[...] = jnp.zeros(acc_sc.shape, F32)

    def scores(c, j):
        kc = k_ref[pl.ds(pl.multiple_of(c * kvb, kvb), kvb), :]
        s_sc[j] = _dot(kc, qt_sc[:, j * Q_BLOCK:(j + 1) * Q_BLOCK])

    for j in range(nq):
        scores(0, j)

    def body(c, carry):
        nxt = jnp.minimum(c + 1, nblk - 1)
        vtc = vt_ref[c]
        for j in range(nq):
            sc = s_sc[j].reshape(kgrp, SUBLANES, Q_BLOCK)
            m_prev = m_sc[j]
            m_new = jnp.maximum(m_prev, _sublane_all(jnp.max(sc, axis=0), jnp.maximum))
            alpha = jnp.exp2(m_prev - m_new)
            p = jnp.exp2(sc - m_new[None])
            l_sc[j] = alpha * l_sc[j] + jnp.sum(p, axis=0)
            pv = _dot(vtc, p.reshape(kvb, Q_BLOCK).astype(BF16))
            scores(nxt, j)
            acc = acc_sc[j].reshape(vgrp, SUBLANES, Q_BLOCK) * alpha[None]
            acc_sc[j] = acc.reshape(VDIM, Q_BLOCK) + pv
            m_sc[j] = m_new
        return carry

    lax.fori_loop(0, nblk, body, 0, unroll=min(KV_UNROLL, nblk))
    for j in range(nq):
        inv = 1.0 / _sublane_all(l_sc[j], jnp.add)
        out_t = (acc_sc[j].reshape(vgrp, SUBLANES, Q_BLOCK) * inv[None]).reshape(VDIM, Q_BLOCK)
        o_ref[j * Q_BLOCK:(j + 1) * Q_BLOCK, :] = out_t.T.astype(o_ref.dtype)


def _attention(u_main, k, vt, cs_tab, sn_tab, b, s):
    t = b * s
    tq = min(s, Q_SUBBLOCKS * Q_BLOCK)
    nq = tq // Q_BLOCK
    qpt = s // tq
    kvb = vt.shape[-1]
    tab = pl.BlockSpec((tq, LANES), lambda bb, h, qi: (bb * qpt + qi, 0))
    kern = functools.partial(_attn_kernel, scale=float(NOPE + ROPE) ** -0.5)
    return pl.pallas_call(
        kern,
        out_shape=jax.ShapeDtypeStruct((t, HEADS * VDIM), BF16),
        grid=(b, HEADS, qpt),
        in_specs=[pl.BlockSpec((tq, NOPE), lambda bb, h, qi: (bb * qpt + qi, h)),
                  pl.BlockSpec((tq, LANES), lambda bb, h, qi: (bb * qpt + qi, OFF_QR // LANES + h // 2)),
                  tab, tab,
                  pl.BlockSpec((None, None, s, 2 * LANES), lambda bb, h, qi: (bb, h, 0, 0)),
                  pl.BlockSpec((None, None, s // kvb, VDIM, kvb), lambda bb, h, qi: (bb, h, 0, 0, 0))],
        out_specs=pl.BlockSpec((tq, VDIM), lambda bb, h, qi: (bb * qpt + qi, h)),
        scratch_shapes=[pltpu.VMEM((2 * LANES, tq), BF16), pltpu.VMEM((nq, kvb, Q_BLOCK), F32),
                        pltpu.VMEM((nq, SUBLANES, Q_BLOCK), F32), pltpu.VMEM((nq, SUBLANES, Q_BLOCK), F32),
                        pltpu.VMEM((nq, VDIM, Q_BLOCK), F32)],
        compiler_params=_params("parallel", "parallel", "arbitrary"),
        name="mla_attention",
    )(u_main, u_main, cs_tab, sn_tab, k, vt)


HALO = BF16_ROWS
CONV_COLS = 512
assert all(o % CONV_COLS == 0 for o in (OFF_XBC, OFF_SCB, OFF_SCC, OFF_SCX))


def _fill_ext(ext_ref, prev, cur, nxt, first, last):
    tm = cur.shape[0]
    ext_ref[:HALO, :] = jnp.where(first, 0.0, prev.astype(F32))
    ext_ref[HALO:HALO + tm, :] = cur.astype(F32)
    ext_ref[HALO + tm:, :] = jnp.where(last, 0.0, nxt.astype(F32))


def _taps(ext_ref, w_ref, ksize, tm):
    pad = ksize // 2
    acc = None
    for kk in range(ksize):
        term = ext_ref[pl.ds(HALO - pad + kk, tm), :] * w_ref[kk:kk + 1, :]
        acc = term if acc is None else acc + term
    return acc


def _ssm_conv_kernel(prev_ref, cur_ref, next_ref, w_ref, b_ref, o_ref, ext_ref, *, tiles_per_seq):
    i = pl.program_id(0)
    tm = cur_ref.shape[0]
    _fill_ext(ext_ref, prev_ref[...], cur_ref[...], next_ref[...],
              i % tiles_per_seq == 0, i % tiles_per_seq == tiles_per_seq - 1)
    o_ref[...] = _silu(_taps(ext_ref, w_ref, SSM_CONV, tm) + b_ref[...]).astype(o_ref.dtype)


def _halo_specs(tm, tc, col0, n_rows):
    r = tm // HALO
    last = n_rows // HALO - 1
    cb = col0 // tc
    return [pl.BlockSpec((HALO, tc), lambda i, j: (jnp.maximum(i * r - 1, 0), cb + j)),
            pl.BlockSpec((tm, tc), lambda i, j: (i, cb + j)),
            pl.BlockSpec((HALO, tc), lambda i, j: (jnp.minimum((i + 1) * r, last), cb + j))]


def _ssm_conv(u_main, conv_w, conv_b, s):
    t = u_main.shape[0]
    tm = min(s, 512)
    tc = CONV_COLS
    kern = functools.partial(_ssm_conv_kernel, tiles_per_seq=s // tm)
    return pl.pallas_call(
        kern,
        out_shape=jax.ShapeDtypeStruct((t, XBC), BF16),
        grid=(t // tm, XBC // tc),
        in_specs=_halo_specs(tm, tc, OFF_XBC, t) + [
            pl.BlockSpec((SSM_CONV, tc), lambda i, j: (0, j)),
            pl.BlockSpec((1, tc), lambda i, j: (0, j))],
        out_specs=pl.BlockSpec((tm, tc), lambda i, j: (i, j)),
        scratch_shapes=[pltpu.VMEM((tm + 2 * HALO, tc), F32)],
        compiler_params=_params("parallel", "parallel"),
        name="ssm_conv",
    )(u_main, u_main, u_main, conv_w, conv_b.reshape(1, XBC))


def _sconv_kernel(cp_ref, cc_ref, cn_ref, xp_ref, xc_ref, xn_ref, b_ref, w_ref, o_ref, ext_ref, *, tiles_per_seq):
    i = pl.program_id(0)
    tm = cc_ref.shape[0]
    f = lambda r: r[...].astype(F32)
    _fill_ext(ext_ref, f(cp_ref) * f(xp_ref), f(cc_ref) * f(xc_ref), f(cn_ref) * f(xn_ref),
              i % tiles_per_seq == 0, i % tiles_per_seq == tiles_per_seq - 1)
    o_ref[...] = (f(b_ref) * _taps(ext_ref, w_ref, SCONV_K, tm)).astype(o_ref.dtype)


def _short_conv(u_main, sconv_w, s):
    t = u_main.shape[0]
    tm = min(s, 512)
    tc = CONV_COLS
    kern = functools.partial(_sconv_kernel, tiles_per_seq=s // tm)
    return pl.pallas_call(
        kern,
        out_shape=jax.ShapeDtypeStruct((t, D), BF16),
        grid=(t // tm, D // tc),
        in_specs=_halo_specs(tm, tc, OFF_SCC, t) + _halo_specs(tm, tc, OFF_SCX, t) + [
            pl.BlockSpec((tm, tc), lambda i, j: (i, OFF_SCB // tc + j)),
            pl.BlockSpec((SCONV_K, tc), lambda i, j: (0, j))],
        out_specs=pl.BlockSpec((tm, tc), lambda i, j: (i, j)),
        scratch_shapes=[pltpu.VMEM((tm + 2 * HALO, tc), F32)],
        compiler_params=_params("parallel", "parallel"),
        name="short_conv",
    )(u_main, u_main, u_main, u_main, u_main, u_main, u_main, sconv_w)


FWD_COL = 2 * SSM_HEADS
BWD_COL = 3 * SSM_HEADS
PAIRS = SSM_HEADS // 2
PAIRS_PER_GROUP = PAIRS // SSM_GROUPS


def _cumsum(x, axis, reverse):
    n = x.shape[axis]
    idx = lax.broadcasted_iota(I32, x.shape, axis)
    k = 1
    while k < n:
        if reverse:
            x = x + jnp.where(idx < n - k, pltpu.roll(x, n - k, axis), 0.0)
        else:
            x = x + jnp.where(idx >= k, pltpu.roll(x, k, axis), 0.0)
        k *= 2
    return x


def _spread_heads(vals, sel, base, exact):
    lane = _lane_iota(vals.shape)
    v = jnp.where(lane >= base, jnp.where(lane < base + SSM_HEADS, vals, 0.0), 0.0)
    hi = v.astype(BF16)
    out = _dot(hi, sel)
    if exact:
        rest = v - hi.astype(F32)
        mid = rest.astype(BF16)
        out = out + _dot(mid, sel) + _dot((rest - mid.astype(F32)).astype(BF16), sel)
    return out


def _ssd_direction(xbc_ref, us_ref, ust_ref, sel, dtb_row, a_row, dtb_col, a_col, d_row, y_ref, st_ref, *, reverse):
    base = BWD_COL if reverse else FWD_COL
    edge = 0 if reverse else CHUNK - 1
    dt_c = _softplus(us_ref[...] + dtb_row)
    cum_c = _cumsum(dt_c * a_row, 0, reverse)
    dt_r = _softplus(ust_ref[...] + dtb_col)
    cum_r = _cumsum(dt_r * a_col, 1, reverse)
    tot = cum_c[edge:edge + 1, :]
    stacked = jnp.concatenate([dt_c, jnp.exp(cum_c), jnp.exp(tot - cum_c) * dt_c], axis=0)
    spread = _spread_heads(stacked, sel, base, False)
    dt_x, ecum_x, dec_x = spread[:CHUNK], spread[CHUNK:2 * CHUNK], spread[2 * CHUNK:]
    etot_x = _spread_heads(jnp.broadcast_to(jnp.exp(tot), (SUBLANES, LANES)), sel, base, True)[0:1, :]

    li = lax.broadcasted_iota(I32, (CHUNK, CHUNK), 0)
    si = lax.broadcasted_iota(I32, (CHUNK, CHUNK), 1)
    mask = (li <= si) if reverse else (li >= si)
    lane = _lane_iota((CHUNK, LANES))
    lo = lane < SSM_P

    for g in range(SSM_GROUPS):
        bg = xbc_ref[:, SSM_INNER + g * SSM_N:SSM_INNER + (g + 1) * SSM_N]
        cg = xbc_ref[:, SSM_INNER + (SSM_GROUPS + g) * SSM_N:SSM_INNER + (SSM_GROUPS + g + 1) * SSM_N]
        cb = _dot_nt(cg, bg)
        for j in range(PAIRS_PER_GROUP):
            pr = g * PAIRS_PER_GROUP + j
            c1, c2 = base + 2 * pr, base + 2 * pr + 1
            cols = slice(pr * LANES, (pr + 1) * LANES)

            xs = xbc_ref[:, cols].astype(F32)
            xdt = xs * dt_x[:, cols]
            scores = []
            for c in (c1, c2):
                seg = cum_c[:, c:c + 1] - cum_r[c:c + 1, :]
                scores.append((cb * jnp.exp(jnp.where(mask, seg, -jnp.inf))).astype(BF16))
            xb = xdt.astype(BF16)
            zero = jnp.zeros_like(xb)
            x_bd = jnp.concatenate([jnp.where(lo, xb, zero), jnp.where(lo, zero, xb)], axis=0)
            y = _dot(jnp.concatenate(scores, axis=1), x_bd)
            st = st_ref[g, :, j * LANES:(j + 1) * LANES]
            y = y + _dot(cg, st.astype(BF16)) * ecum_x[:, cols]
            if not reverse:
                y = y + xs * d_row[:, cols]
            y_ref[:, cols] = y.astype(y_ref.dtype)
            new = _dot_tn(bg, (xs * dec_x[:, cols]).astype(BF16))
            st_ref[g, :, j * LANES:(j + 1) * LANES] = st * etot_x[:, cols] + new


def _ssd_kernel(xf_ref, usf_ref, ustf_ref, xb_ref, usb_ref, ustb_ref, sel_ref, dtb_row_ref, alog_row_ref,
                dtb_col_ref, alog_col_ref, d_ref, yf_ref, yb_ref, stf_ref, stb_ref):
    @pl.when(pl.program_id(1) == 0)
    def _():
        stf_ref[...] = jnp.zeros(stf_ref.shape, F32)
        stb_ref[...] = jnp.zeros(stb_ref.shape, F32)

    dtb_row, dtb_col = dtb_row_ref[...], dtb_col_ref[...]
    a_row, a_col = -jnp.exp(alog_row_ref[...]), -jnp.exp(alog_col_ref[...])
    d_row = d_ref[...]
    sel = sel_ref[...]
    _ssd_direction(xf_ref, usf_ref, ustf_ref, sel, dtb_row, a_row, dtb_col, a_col, d_row, yf_ref, stf_ref,
                   reverse=False)
    _ssd_direction(xb_ref, usb_ref, ustb_ref, sel, dtb_row, a_row, dtb_col, a_col, d_row, yb_ref, stb_ref,
                   reverse=True)


def _ssd_scan(xbc_conv, u_small, u_small_t, a_log, dt_bias, d_skip, b, s):
    t = b * s
    nc = s // CHUNK
    pad = jnp.zeros((2 * SSM_HEADS,), F32)
    dtb = jnp.concatenate([pad, dt_bias.reshape(-1)])
    alog = jnp.concatenate([pad, a_log.reshape(-1)])
    d_row = jnp.repeat(d_skip, SSM_P)[None, :]
    head_of_lane = np.concatenate([np.full(2 * SSM_HEADS, -1), np.arange(SSM_HEADS), np.arange(SSM_HEADS)])
    sel = jnp.asarray(head_of_lane[:, None] == np.repeat(np.arange(SSM_HEADS), SSM_P)[None, :], BF16)
    fw = lambda bb, i: bb * nc + i
    bw = lambda bb, i: bb * nc + nc - 1 - i
    row = lambda w: pl.BlockSpec((1, w), lambda bb, i: (0, 0))
    col = pl.BlockSpec((LANES, 1), lambda bb, i: (0, 0))

    def chunk_specs(idx):
        return [pl.BlockSpec((CHUNK, XBC), lambda bb, i: (idx(bb, i), 0)),
                pl.BlockSpec((CHUNK, LANES), lambda bb, i: (idx(bb, i), 0)),
                pl.BlockSpec((LANES, CHUNK), lambda bb, i: (0, idx(bb, i)))]

    return pl.pallas_call(
        _ssd_kernel,
        out_shape=(jax.ShapeDtypeStruct((t, SSM_INNER), BF16),) * 2,
        grid=(b, nc),
        in_specs=chunk_specs(fw) + chunk_specs(bw) + [
            pl.BlockSpec((LANES, SSM_INNER), lambda bb, i: (0, 0)),
            row(LANES), row(LANES), col, col, row(SSM_INNER)],
        out_specs=(pl.BlockSpec((CHUNK, SSM_INNER), lambda bb, i: (fw(bb, i), 0)),
                   pl.BlockSpec((CHUNK, SSM_INNER), lambda bb, i: (bw(bb, i), 0))),
        scratch_shapes=[pltpu.VMEM((SSM_GROUPS, SSM_N, SSM_INNER // SSM_GROUPS), F32)] * 2,
        compiler_params=_params("parallel", "arbitrary"),
        name="ssd_scan",
    )(xbc_conv, u_small, u_small_t, xbc_conv, u_small, u_small_t, sel,
      dtb[None, :], alog[None, :], dtb[:, None], alog[:, None], d_row)


def _ssm_post_kernel(yf_ref, yb_ref, z_ref, w_ref, o_ref):
    y = (yf_ref[...].astype(F32) + yb_ref[...].astype(F32)) * _silu(z_ref[...].astype(F32))
    o_ref[...] = _rms(y, w_ref[...]).astype(o_ref.dtype)


def _ssm_post(y_fwd, y_bwd, u_main, norm_w):
    t = y_fwd.shape[0]
    tm = min(t, 512)
    gw = SSM_INNER // SSM_GROUPS
    blk = pl.BlockSpec((tm, gw), lambda i, g: (i, g))
    return pl.pallas_call(
        _ssm_post_kernel,
        out_shape=jax.ShapeDtypeStruct((t, SSM_INNER), BF16),
        grid=(t // tm, SSM_GROUPS),
        in_specs=[blk, blk, pl.BlockSpec((tm, gw), lambda i, g: (i, OFF_Z // gw + g)),
                  pl.BlockSpec((1, gw), lambda i, g: (0, g))],
        out_specs=blk,
        compiler_params=_params("parallel", "parallel"),
        name="ssm_gate_norm",
    )(y_fwd, y_bwd, u_main, norm_w.reshape(1, SSM_INNER))


def _merge_kernel(ya_ref, ys_ref, yc_ref, pa_ref, ps_ref, pc_ref, ga_ref, gs_ref, gc_ref, o_ref):
    g = lambda r: _sigmoid(r[...].astype(F32))
    m = (g(ga_ref) * _dot(ya_ref[...], pa_ref[...]) + g(gs_ref) * _dot(ys_ref[...], ps_ref[...])
         + g(gc_ref) * _dot(yc_ref[...], pc_ref[...]))
    o_ref[...] = m.astype(o_ref.dtype)


def _merge(y_att, y_ssm, y_conv, w_branch, u_main):
    t = y_att.shape[0]
    tm = min(t, 512)
    tn = 512
    yb = pl.BlockSpec((tm, D), lambda i, j: (i, 0))
    pspec = lambda r: pl.BlockSpec((D, tn), lambda i, j: (r, j))
    gspec = lambda r: pl.BlockSpec((tm, tn), lambda i, j: (i, OFF_GATE // tn + r * (D // tn) + j))
    return pl.pallas_call(
        _merge_kernel,
        out_shape=jax.ShapeDtypeStruct((t, D), BF16),
        grid=(t // tm, D // tn),
        in_specs=[yb, yb, yb, pspec(0), pspec(1), pspec(2), gspec(0), gspec(1), gspec(2)],
        out_specs=pl.BlockSpec((tm, tn), lambda i, j: (i, j)),
        compiler_params=_params("parallel", "arbitrary"),
        name="branch_merge",
    )(y_att, y_ssm, y_conv, w_branch, w_branch, w_branch, u_main, u_main, u_main)


def _split_bf16(x):
    hi = x.astype(BF16)
    lo = (x - hi.astype(F32)).astype(BF16)
    return hi, lo


def _wo_kernel(m_ref, h_ref, wo_ref, nw_ref, rhi_ref, rlo_ref, rb_ref, h1_ref, t_ref, lg_ref):
    h1 = h_ref[...] + _dot(m_ref[...], wo_ref[...])
    h1_ref[...] = h1
    tn = _rms(h1, nw_ref[...])
    for sl in range(SLAB):
        t_ref[pl.ds(sl, tn.shape[0], stride=SLAB), :] = tn[:, sl * LANES:(sl + 1) * LANES]
    hi, lo = _split_bf16(tn)
    lg_ref[...] = _dot(hi, rhi_ref[...]) + _dot(lo, rhi_ref[...]) + _dot(hi, rlo_ref[...]) + rb_ref[...]


def _out_proj(merged, h, w_o, ffn_norm_w, r_hi, r_lo, r_bias):
    t = h.shape[0]
    tm = min(t, 256)
    blk = pl.BlockSpec((tm, D), lambda i: (i, 0))
    const = lambda shape: pl.BlockSpec(shape, lambda i: (0, 0))
    lg = pl.BlockSpec((tm, LANES), lambda i: (i, 0))
    return pl.pallas_call(
        _wo_kernel,
        out_shape=(jax.ShapeDtypeStruct((t, D), F32), jax.ShapeDtypeStruct((t * SLAB, LANES), F32),
                   jax.ShapeDtypeStruct((t, LANES), F32)),
        grid=(t // tm,),
        in_specs=[blk, blk, const((D, D)), const((1, D)), const((D, LANES)), const((D, LANES)),
                  const((1, LANES))],
        out_specs=(blk, pl.BlockSpec((tm * SLAB, LANES), lambda i: (i, 0)), lg),
        compiler_params=_params("parallel"),
        name="out_proj_router",
    )(merged, h, w_o, ffn_norm_w.reshape(1, D), r_hi, r_lo, r_bias)


def _route_kernel(lg_ref, id_ref, wt_ref, rk_ref, cnt_ref, carry_sc):
    @pl.when(pl.program_id(0) == 0)
    def _():
        carry_sc[...] = jnp.zeros(carry_sc.shape, F32)

    lg = lg_ref[...]
    lane = _lane_iota(lg.shape).astype(F32)
    big = float(LANES)
    ninf = -jnp.inf
    first = lambda hit: jnp.min(jnp.where(hit, lane, big), axis=1, keepdims=True)

    gl = jnp.where(lane < N_GROUPS, lg, ninf)
    gmax = jnp.max(gl, axis=1, keepdims=True)
    gidx = first(gl == gmax)
    g_w = 1.0 / jnp.sum(jnp.exp(gl - gmax), axis=1, keepdims=True)

    in_group = (lane >= N_GROUPS + gidx * EPG) & (lane < N_GROUPS + (gidx + 1.0) * EPG)
    el = jnp.where(in_group, lg, ninf)
    v0 = jnp.max(el, axis=1, keepdims=True)
    i0 = first(el == v0)
    el = jnp.where(lane == i0, ninf, el)
    v1 = jnp.max(el, axis=1, keepdims=True)
    i1 = first(el == v1)
    e1 = jnp.exp(v1 - v0)
    w0 = g_w / (1.0 + e1)
    w1 = g_w * e1 / (1.0 + e1)
    id_ref[...] = jnp.where(lane == 0.0, i0, i1).astype(I32) - N_GROUPS
    wt_ref[...] = jnp.where(lane == 0.0, w0, w1)

    hit0 = lane == i0 - N_GROUPS
    hit1 = lane == i1 - N_GROUPS
    onehot = jnp.where(hit0, 1.0, jnp.where(hit1, 1.0, 0.0))
    tm = lg.shape[0]
    earlier = lax.broadcasted_iota(I32, (tm, tm), 1) < lax.broadcasted_iota(I32, (tm, tm), 0)
    carry = carry_sc[0:1, :]
    before = _dot(jnp.where(earlier, 1.0, 0.0).astype(BF16), onehot.astype(BF16)) + carry
    r0 = jnp.sum(jnp.where(hit0, before, 0.0), axis=1, keepdims=True)
    r1 = jnp.sum(jnp.where(hit1, before, 0.0), axis=1, keepdims=True)
    rk_ref[...] = jnp.where(lane == 0.0, r0, r1).astype(I32)
    carry_sc[...] = jnp.broadcast_to(carry + jnp.sum(onehot, axis=0, keepdims=True), carry_sc.shape)
    cnt_ref[...] = carry_sc[...]


def _route(logits):
    t = logits.shape[0]
    tm = min(t, 512)
    blk = pl.BlockSpec((tm, LANES), lambda i: (i, 0))
    cnt = pl.BlockSpec((SUBLANES, LANES), lambda i: (0, 0))
    return pl.pallas_call(
        _route_kernel,
        out_shape=(jax.ShapeDtypeStruct((t, LANES), I32), jax.ShapeDtypeStruct((t, LANES), F32),
                   jax.ShapeDtypeStruct((t, LANES), I32), jax.ShapeDtypeStruct((SUBLANES, LANES), F32)),
        grid=(t // tm,),
        in_specs=[blk],
        out_specs=(blk, blk, blk, cnt),
        scratch_shapes=[pltpu.VMEM((SUBLANES, LANES), F32)],
        compiler_params=_params("arbitrary"),
        name="route",
    )(logits)


def _dispatch_plan(ids, wts, ranks, counts, t, tile):
    n_rows = 2 * t
    n_tiles = n_rows // tile + N_EXPERTS
    n_pos = n_tiles * tile
    e = ids[:, :2].reshape(-1)
    w = wts[:, :2].reshape(-1)
    rank = ranks[:, :2].reshape(-1)
    counts = counts[0, :N_EXPERTS].astype(I32)
    padded = ((counts + tile - 1) // tile) * tile
    ends = jnp.cumsum(padded)
    starts = ends - padded
    pos = starts[e] + rank
    r = jnp.arange(n_rows, dtype=I32)
    packed = jnp.stack([r, lax.bitcast_convert_type(w, I32)], axis=1)
    slots = jnp.full((n_pos, 2), -1, I32).at[pos].set(packed)
    src = slots[:, 0]
    valid = src >= 0
    tile_start = jnp.arange(n_tiles, dtype=I32) * tile
    tile_e = jnp.minimum(jnp.sum(tile_start[:, None] >= ends[None, :], axis=1), N_EXPERTS - 1).astype(I32)
    in_tile = jnp.arange(n_pos, dtype=I32) % tile
    row_tok = jnp.where(valid, src // 2, 0)
    row_dst = jnp.where(valid, (src % 2) * t + src // 2, n_rows + in_tile)
    row_w = jnp.where(valid, lax.bitcast_convert_type(slots[:, 1], F32), 0.0)
    n_used = (ends[-1] // tile).astype(I32).reshape(1)
    return (row_tok.reshape(n_tiles, 1, tile), row_dst.reshape(n_tiles, 1, tile),
            row_w.reshape(n_pos, 1), tile_e, n_used)


DMA_UNROLL = 8


def _moe_kernel(te_ref, nu_ref, tok_ref, tokn_ref, dstp_ref, dst_ref, rw_ref, t_hbm, wg_ref, wu_ref, wd_ref, o_hbm,
                xbuf, ybuf, wg_sc, wu_sc, wd_sc, gsem, ssem):
    i = pl.program_id(0)
    n_used = nu_ref[0]
    tile = rw_ref.shape[0]
    rows = tile * PITCH
    slot = i % 2

    def gather_copy(tok, r, slot_):
        dst = xbuf.at[pl.ds(pl.multiple_of(slot_ * rows + r * PITCH, SUBLANES), SLAB), :]
        return pltpu.make_async_copy(t_hbm.at[tok[0, 0, r]], dst, gsem.at[slot_])

    def scatter_copy(dst, r):
        src = ybuf.at[pl.ds(pl.multiple_of(r * PITCH, SUBLANES), SLAB), :]
        return pltpu.make_async_copy(src, o_hbm.at[dst[0, 0, r]], ssem)

    def wait_rows(buf, sem):
        done = buf.at[pl.ds(0, tile * SLAB), :]
        pltpu.make_async_copy(done, done, sem).wait()

    def looped(make):
        def body(g, c):
            for j in range(DMA_UNROLL):
                make(g * DMA_UNROLL + j).start(priority=j % 2)
            return c
        lax.fori_loop(0, tile // DMA_UNROLL, body, 0)

    @pl.when(i == 0)
    def _():
        ybuf[...] = jnp.zeros(ybuf.shape, F32)
        looped(lambda r: gather_copy(tok_ref, r, 0))

    @pl.when(i < n_used)
    def _():
        @pl.when((i == 0) | (te_ref[i] != te_ref[jnp.maximum(i - 1, 0)]))
        def _():
            wg_sc[...] = wg_ref[...].astype(BF16)
            wu_sc[...] = wu_ref[...].astype(BF16)
            wd_sc[...] = wd_ref[...].astype(BF16)

        @pl.when(i + 1 < n_used)
        def _():
            looped(lambda r: gather_copy(tokn_ref, r, 1 - slot))

        wait_rows(xbuf, gsem.at[slot])
        base = slot * rows
        x = jnp.concatenate([xbuf[pl.ds(base + sl, tile, stride=PITCH), :] for sl in range(SLAB)],
                            axis=1).astype(BF16)
        for r in range(tile):
            scatter_copy(dstp_ref, r).start(priority=r % 2)
        gate = _dot(x, wg_sc[...])
        mid = (_silu(gate) * _dot(x, wu_sc[...])).astype(BF16)
        y = _dot(mid, wd_sc[...]) * rw_ref[...]
        wait_rows(ybuf, ssem)
        for sl in range(SLAB):
            ybuf[pl.ds(sl, tile, stride=PITCH), :] = y[:, sl * LANES:(sl + 1) * LANES]

        @pl.when(i == n_used - 1)
        def _():
            looped(lambda r: scatter_copy(dst_ref, r))
            wait_rows(ybuf, ssem)


def _moe(t_norm, plan, layer, w_gate, w_up, w_down):
    t = t_norm.shape[0] // SLAB
    row_tok, row_dst, row_w, tile_e, n_used = plan
    n_tiles, _, tile = row_tok.shape
    t_slab = t_norm.reshape(t, SLAB, LANES)
    extra = (2 * t + jnp.arange(tile, dtype=I32)).reshape(1, 1, tile)
    dst_prev = jnp.concatenate([extra, row_dst[:-1]], axis=0)
    smem = lambda f: pl.BlockSpec((1, 1, tile), lambda i, te, nu: (f(i), 0, 0), memory_space=pltpu.SMEM)
    wspec = lambda a, b: pl.BlockSpec((None, None, a, b), lambda i, te, nu: (layer, te[i], 0, 0))
    out = pl.pallas_call(
        _moe_kernel,
        out_shape=jax.ShapeDtypeStruct((2 * t + tile, SLAB, LANES), F32),
        grid_spec=pltpu.PrefetchScalarGridSpec(
            num_scalar_prefetch=2,
            grid=(n_tiles,),
            in_specs=[smem(lambda i: i), smem(lambda i: jnp.minimum(i + 1, n_tiles - 1)),
                      smem(lambda i: i), smem(lambda i: i),
                      pl.BlockSpec((tile, 1), lambda i, te, nu: (i, 0)),
                      pl.BlockSpec(memory_space=pl.ANY),
                      wspec(D, D_EXPERT), wspec(D, D_EXPERT), wspec(D_EXPERT, D)],
            out_specs=pl.BlockSpec(memory_space=pl.ANY),
            scratch_shapes=[pltpu.VMEM((2 * tile * PITCH, LANES), F32), pltpu.VMEM((tile * PITCH, LANES), F32),
                            pltpu.VMEM((D, D_EXPERT), BF16), pltpu.VMEM((D, D_EXPERT), BF16),
                            pltpu.VMEM((D_EXPERT, D), BF16),
                            pltpu.SemaphoreType.DMA((2,)), pltpu.SemaphoreType.DMA(())]),
        compiler_params=_params("arbitrary"),
        name="moe_experts",
    )(tile_e, n_used, row_tok, row_tok, dst_prev, row_dst, row_w, t_slab, w_gate, w_up, w_down)
    return out.reshape((2 * t + tile) * SLAB, LANES)


def kernel(x, positions, attn_norm_w, w_in, kv_norm_w, w_uk, w_uv, ssm_conv_w, ssm_conv_b, ssm_A_log, ssm_dt_bias, ssm_D, ssm_norm_w, sconv_w, w_branch, w_o, ffn_norm_w, router_group_w, router_group_b, router_expert_w, router_expert_b, expert_w_gate, expert_w_up, expert_w_down, final_norm_w):
    b, s, _ = x.shape
    t = b * s
    depth = w_in.shape[0]
    h = x.reshape(t, D)
    cs_tab, sn_tab = _rope_tables(positions)
    moe_out = None
    for l in range(depth):
        if moe_out is None:
            n = _norm(h, attn_norm_w[l], BF16)
        else:
            h, n = _add_norm(h, moe_out, attn_norm_w[l], BF16)
        w_perm = _permute_in_proj(w_in[l])
        w_small = w_perm[:, OFF_SMALL:OFF_SMALL + LANES]
        u_main, u_small, u_small_t = _in_proj(n, w_perm, w_small, w_small.T)

        k, vt = _kv_proj(u_main, u_small, kv_norm_w[l], w_uk[l].astype(BF16), w_uv[l].T.astype(BF16),
                         cs_tab, sn_tab, b, s)
        y_att = _attention(u_main, k, vt, cs_tab, sn_tab, b, s)

        xbc_conv = _ssm_conv(u_main, ssm_conv_w[l], ssm_conv_b[l], s)
        y_fwd, y_bwd = _ssd_scan(xbc_conv, u_small, u_small_t, ssm_A_log[l], ssm_dt_bias[l], ssm_D[l], b, s)
        y_ssm = _ssm_post(y_fwd, y_bwd, u_main, ssm_norm_w[l])

        y_conv = _short_conv(u_main, sconv_w[l], s)

        merged = _merge(y_att, y_ssm, y_conv, w_branch[l].astype(BF16), u_main)

        r_w = jnp.concatenate([router_group_w[l], router_expert_w[l],
                               jnp.zeros((D, LANES - N_GROUPS - N_EXPERTS), F32)], axis=1)
        r_b = jnp.concatenate([router_group_b[l], router_expert_b[l],
                               jnp.zeros((LANES - N_GROUPS - N_EXPERTS,), F32)])[None, :]
        r_hi, r_lo = _split_bf16(r_w)
        h, t_norm, logits = _out_proj(merged, h, w_o[l].astype(BF16), ffn_norm_w[l], r_hi, r_lo, r_b)

        ids, wts, ranks, counts = _route(logits)
        plan = _dispatch_plan(ids, wts, ranks, counts, t, MOE_TILE)
        moe_out = _moe(t_norm, plan, l, expert_w_gate, expert_w_up, expert_w_down)

    h, out = _add_norm(h, moe_out, final_norm_w, F32)
    return out.reshape(b, s, D)
```

```python
import functools

import numpy as np
import jax
import jax.numpy as jnp
from jax import lax
from jax.experimental import pallas as pl
from jax.experimental.pallas import tpu as pltpu

F32 = jnp.float32
BF16 = jnp.bfloat16
I32 = jnp.int32

D = 2048
EPS = 1e-6
LOG2_E = 1.4426950408889634
HEADS = 16
NOPE = 128
ROPE = 64
VDIM = 128
KV_RANK = 512
ROPE_THETA = 10000.0
SSM_HEADS = 32
SSM_P = 64
SSM_INNER = SSM_HEADS * SSM_P
SSM_GROUPS = 4
SSM_N = 128
SSM_CONV = 5
CHUNK = 128
XBC = SSM_INNER + 2 * SSM_GROUPS * SSM_N
SCONV_K = 3
N_GROUPS = 8
EPG = 4
N_EXPERTS = N_GROUPS * EPG
D_EXPERT = 512
Q_DIM = HEADS * (NOPE + ROPE)
IN_COLS = Q_DIM + KV_RANK + ROPE + SSM_INNER + XBC + 2 * SSM_HEADS + 3 * D + 3 * D

LANES = 128
SUBLANES = 8
BF16_ROWS = 16
V7X_VMEM_BYTES = 64 * 1024 * 1024
VMEM_LIMIT = 56 * 1024 * 1024

OFF_QN = 0
OFF_QR = OFF_QN + HEADS * NOPE
OFF_CKV = OFF_QR + HEADS * ROPE
OFF_Z = OFF_CKV + KV_RANK
OFF_XBC = OFF_Z + SSM_INNER
OFF_SCB = OFF_XBC + XBC
OFF_SCC = OFF_SCB + D
OFF_SCX = OFF_SCC + D
OFF_GATE = OFF_SCX + D
OFF_SMALL = OFF_GATE + 3 * D
assert OFF_SMALL + LANES == IN_COLS
IN_PROJ_TN = 1024
U_COLS = -(-IN_COLS // IN_PROJ_TN) * IN_PROJ_TN

SLAB = D // LANES
PITCH = 24
MOE_TILE = 384


def _permute_in_proj(w):
    o_ckv = Q_DIM
    o_kr = o_ckv + KV_RANK
    o_z = o_kr + ROPE
    o_dt = o_z + SSM_INNER + XBC
    o_rest = o_dt + 2 * SSM_HEADS
    q = w[:, :Q_DIM].reshape(D, HEADS, NOPE + ROPE)
    parts = [q[:, :, :NOPE].reshape(D, HEADS * NOPE), q[:, :, NOPE:].reshape(D, HEADS * ROPE),
             w[:, o_ckv:o_kr], w[:, o_z:o_dt], w[:, o_rest:], w[:, o_kr:o_z], w[:, o_dt:o_rest]]
    parts.append(jnp.zeros((D, U_COLS - IN_COLS), w.dtype))
    out = jnp.concatenate([p.astype(BF16) for p in parts], axis=1)
    assert out.shape == (D, U_COLS)
    return out


def _params(*sem):
    return pltpu.CompilerParams(dimension_semantics=tuple(sem), vmem_limit_bytes=VMEM_LIMIT)


def _dot(a, b):
    return jnp.dot(a, b, preferred_element_type=F32)


def _dot_nt(a, b):
    return lax.dot_general(a, b, (((1,), (1,)), ((), ())), preferred_element_type=F32)


def _dot_tn(a, b):
    return lax.dot_general(a, b, (((0,), (0,)), ((), ())), preferred_element_type=F32)


def _sigmoid(x):
    return 1.0 / (1.0 + jnp.exp(-x))


def _silu(x):
    return x * _sigmoid(x)


def _softplus(x):
    return jnp.maximum(x, 0.0) + jnp.log(1.0 + jnp.exp(-jnp.abs(x)))


def _lane_iota(shape):
    return lax.broadcasted_iota(I32, shape, len(shape) - 1)


def _rope_partner(v):
    lane = _lane_iota(v.shape)
    return jnp.where((lane & 32) == 0, pltpu.roll(v, LANES - 32, 1), pltpu.roll(v, 32, 1))


def _rope_tab_kernel(pos_ref, freq_ref, sign_ref, cs_ref, sn_ref):
    ang = pos_ref[...].astype(F32) * freq_ref[...]
    cs_ref[...] = jnp.cos(ang)
    sn_ref[...] = jnp.sin(ang) * sign_ref[...]


def _rope_tables(positions):
    t = positions.size
    tm = min(t, 1024)
    freqs = ROPE_THETA ** (-jnp.arange(0, ROPE, 2, dtype=F32) / ROPE)
    freq_row = jnp.tile(freqs, 4)[None, :]
    sign_row = jnp.tile(jnp.concatenate([-jnp.ones(32, F32), jnp.ones(32, F32)]), 2)[None, :]
    row = pl.BlockSpec((1, LANES), lambda i: (0, 0))
    tab = pl.BlockSpec((tm, LANES), lambda i: (i, 0))
    return pl.pallas_call(
        _rope_tab_kernel,
        out_shape=(jax.ShapeDtypeStruct((t, LANES), F32),) * 2,
        grid=(t // tm,),
        in_specs=[pl.BlockSpec((tm, 1), lambda i: (i, 0)), row, row],
        out_specs=(tab, tab),
        compiler_params=_params("parallel"),
        name="rope_tables",
    )(positions.reshape(t, 1), freq_row, sign_row)


def _rms(x, w):
    return x * lax.rsqrt(jnp.mean(x * x, axis=-1, keepdims=True) + EPS) * w


def _norm_kernel(h_ref, w_ref, n_ref):
    n_ref[...] = _rms(h_ref[...], w_ref[...]).astype(n_ref.dtype)


def _rows_from_slabs(ref, tm):
    return jnp.concatenate([ref[pl.ds(sl, tm, stride=SLAB), :] for sl in range(SLAB)], axis=1)


def _add_norm_kernel(h_ref, o0_ref, o1_ref, w_ref, hs_ref, n_ref):
    tm = h_ref.shape[0]
    hs = h_ref[...] + _rows_from_slabs(o0_ref, tm) + _rows_from_slabs(o1_ref, tm)
    hs_ref[...] = hs
    n_ref[...] = _rms(hs, w_ref[...]).astype(n_ref.dtype)


def _norm(h, w, out_dtype):
    t = h.shape[0]
    tm = min(t, 256)
    blk = pl.BlockSpec((tm, D), lambda i: (i, 0))
    return pl.pallas_call(
        _norm_kernel,
        out_shape=jax.ShapeDtypeStruct((t, D), out_dtype),
        grid=(t // tm,),
        in_specs=[blk, pl.BlockSpec((1, D), lambda i: (0, 0))],
        out_specs=blk,
        compiler_params=_params("parallel"),
        name="rmsnorm",
    )(h, w.reshape(1, D))


def _add_norm(h, o, w, out_dtype):
    t = h.shape[0]
    tm = min(t, 256)
    nt = t // tm
    blk = pl.BlockSpec((tm, D), lambda i: (i, 0))
    return pl.pallas_call(
        _add_norm_kernel,
        out_shape=(jax.ShapeDtypeStruct((t, D), F32), jax.ShapeDtypeStruct((t, D), out_dtype)),
        grid=(nt,),
        in_specs=[blk, pl.BlockSpec((tm * SLAB, LANES), lambda i: (i, 0)),
                  pl.BlockSpec((tm * SLAB, LANES), lambda i: (nt + i, 0)),
                  pl.BlockSpec((1, D), lambda i: (0, 0))],
        out_specs=(blk, blk),
        compiler_params=_params("parallel"),
        name="add_rmsnorm",
    )(h, o, o, w.reshape(1, D))


def _inproj_kernel(n_ref, wm_ref, ws_ref, wst_ref, um_ref, us_ref, ust_ref):
    x = n_ref[...]
    um_ref[...] = _dot(x, wm_ref[...]).astype(um_ref.dtype)

    @pl.when(pl.program_id(1) == 0)
    def _():
        us_ref[...] = _dot(x, ws_ref[...])
        ust_ref[...] = _dot_nt(wst_ref[...], x)


def _in_proj(n, w_main, w_small, w_small_t):
    t = n.shape[0]
    tm = min(t, 2048)
    tn = IN_PROJ_TN
    return pl.pallas_call(
        _inproj_kernel,
        out_shape=(jax.ShapeDtypeStruct((t, U_COLS), BF16),
                   jax.ShapeDtypeStruct((t, LANES), F32),
                   jax.ShapeDtypeStruct((LANES, t), F32)),
        grid=(t // tm, U_COLS // tn),
        in_specs=[pl.BlockSpec((tm, D), lambda i, j: (i, 0)),
                  pl.BlockSpec((D, tn), lambda i, j: (0, j)),
                  pl.BlockSpec((D, LANES), lambda i, j: (0, 0)),
                  pl.BlockSpec((LANES, D), lambda i, j: (0, 0))],
        out_specs=(pl.BlockSpec((tm, tn), lambda i, j: (i, j)),
                   pl.BlockSpec((tm, LANES), lambda i, j: (i, 0)),
                   pl.BlockSpec((LANES, tm), lambda i, j: (0, i))),
        compiler_params=_params("parallel", "arbitrary"),
        name="in_proj",
    )(n, w_main, w_small, w_small_t)


KV_BLOCK = 1024


def _kv_kernel(ckv_ref, nw_ref, wuk_ref, wuvt_ref, us_ref, cs_ref, sn_ref, k_ref, vt_ref):
    nblk, kvb = vt_ref.shape[1], vt_ref.shape[3]
    c = _rms(ckv_ref[...].astype(F32), nw_ref[...]).astype(BF16)
    kn = _dot(c, wuk_ref[...]).astype(k_ref.dtype)
    vt = _dot_nt(wuvt_ref[...], c).astype(vt_ref.dtype)
    u = us_ref[...]
    roped = u * cs_ref[...] + _rope_partner(u) * sn_ref[...]
    lane = _lane_iota(roped.shape)
    even = jnp.where(lane < ROPE, roped, 0.0).astype(k_ref.dtype)
    odd = jnp.where(lane >= ROPE, pltpu.roll(roped, ROPE, 1), 0.0).astype(k_ref.dtype)
    for hh in range(HEADS):
        k_ref[hh, :, :NOPE] = kn[:, hh * NOPE:(hh + 1) * NOPE]
        k_ref[hh, :, NOPE:] = even if hh % 2 == 0 else odd
        for j in range(nblk):
            vt_ref[hh, j] = vt[hh * VDIM:(hh + 1) * VDIM, j * kvb:(j + 1) * kvb]


def _kv_proj(u_main, u_small, kv_norm_w, w_uk, w_uv_t, cs_tab, sn_tab, b, s):
    t = b * s
    tm = min(s, 512)
    spt = s // tm
    kvb = min(s, KV_BLOCK)
    if tm >= kvb:
        vt_spec = pl.BlockSpec((None, HEADS, tm // kvb, VDIM, kvb), lambda i: (i // spt, 0, i % spt, 0, 0))
    else:
        per = kvb // tm
        vt_spec = pl.BlockSpec((None, HEADS, 1, VDIM, tm),
                               lambda i: (i // spt, 0, (i % spt) // per, 0, (i % spt) % per))
    tab = pl.BlockSpec((tm, LANES), lambda i: (i, 0))
    const = lambda shape: pl.BlockSpec(shape, lambda i: (0, 0))
    return pl.pallas_call(
        _kv_kernel,
        out_shape=(jax.ShapeDtypeStruct((b, HEADS, s, 2 * LANES), BF16),
                   jax.ShapeDtypeStruct((b, HEADS, s // kvb, VDIM, kvb), BF16)),
        grid=(t // tm,),
        in_specs=[pl.BlockSpec((tm, KV_RANK), lambda i: (i, OFF_CKV // KV_RANK)),
                  const((1, KV_RANK)), const((KV_RANK, HEADS * NOPE)), const((HEADS * VDIM, KV_RANK)),
                  tab, tab, tab],
        out_specs=(pl.BlockSpec((None, HEADS, tm, 2 * LANES), lambda i: (i // spt, 0, i % spt, 0)), vt_spec),
        compiler_params=_params("parallel"),
        name="kv_proj",
    )(u_main, kv_norm_w.reshape(1, KV_RANK), w_uk, w_uv_t, u_small, cs_tab, sn_tab)


Q_BLOCK = 256
Q_SUBBLOCKS = 4
KV_UNROLL = 4


def _sublane_all(x, op):
    for shift in (4, 2, 1):
        x = op(x, pltpu.roll(x, shift, 0))
    return x


def _attn_kernel(qn_ref, qr_ref, cs_ref, sn_ref, k_ref, vt_ref, o_ref, qt_sc, s_sc, m_sc, l_sc, acc_sc, *, scale):
    nq = m_sc.shape[0]
    nblk, _, kvb = vt_ref.shape
    kgrp = kvb // SUBLANES
    vgrp = VDIM // SUBLANES
    qr = qr_ref[...].astype(F32)
    roped = qr * cs_ref[...] + _rope_partner(qr) * sn_ref[...]
    q = jnp.concatenate([qn_ref[...].astype(F32), roped], axis=1) * (scale * LOG2_E)
    qt_sc[...] = q.T.astype(BF16)
    m_sc[...] = jnp.full(m_sc.shape, -jnp.inf, F32)
    l_sc[...] = jnp.zeros(l_sc.shape, F32)
    acc_sc[...] = jnp.zeros(acc_sc.shape, F32)

    def scores(c, j):
        kc = k_ref[pl.ds(pl.multiple_of(c * kvb, kvb), kvb), :]
        s_sc[j] = _dot(kc, qt_sc[:, j * Q_BLOCK:(j + 1) * Q_BLOCK])

    for j in range(nq):
        scores(0, j)

    def body(c, carry):
        nxt = jnp.minimum(c + 1, nblk - 1)
        vtc = vt_ref[c]
        for j in range(nq):
            sc = s_sc[j].reshape(kgrp, SUBLANES, Q_BLOCK)
            m_prev = m_sc[j]
            m_new = jnp.maximum(m_prev, _sublane_all(jnp.max(sc, axis=0), jnp.maximum))
            alpha = jnp.exp2(m_prev - m_new)
            p = jnp.exp2(sc - m_new[None])
            l_sc[j] = alpha * l_sc[j] + jnp.sum(p, axis=0)
            pv = _dot(vtc, p.reshape(kvb, Q_BLOCK).astype(BF16))
            scores(nxt, j)
            acc = acc_sc[j].reshape(vgrp, SUBLANES, Q_BLOCK) * alpha[None]
            acc_sc[j] = acc.reshape(VDIM, Q_BLOCK) + pv
            m_sc[j] = m_new
        return carry

    lax.fori_loop(0, nblk, body, 0, unroll=min(KV_UNROLL, nblk))
    for j in range(nq):
        inv = 1.0 / _sublane_all(l_sc[j], jnp.add)
        out_t = (acc_sc[j].reshape(vgrp, SUBLANES, Q_BLOCK) * inv[None]).reshape(VDIM, Q_BLOCK)
        o_ref[j * Q_BLOCK:(j + 1) * Q_BLOCK, :] = out_t.T.astype(o_ref.dtype)


def _attention(u_main, k, vt, cs_tab, sn_tab, b, s):
    t = b * s
    tq = min(s, Q_SUBBLOCKS * Q_BLOCK)
    nq = tq // Q_BLOCK
    qpt = s // tq
    kvb = vt.shape[-1]
    tab = pl.BlockSpec((tq, LANES), lambda bb, h, qi: (bb * qpt + qi, 0))
    kern = functools.partial(_attn_kernel, scale=float(NOPE + ROPE) ** -0.5)
    return pl.pallas_call(
        kern,
        out_shape=jax.ShapeDtypeStruct((t, HEADS * VDIM), BF16),
        grid=(b, HEADS, qpt),
        in_specs=[pl.BlockSpec((tq, NOPE), lambda bb, h, qi: (bb * qpt + qi, h)),
                  pl.BlockSpec((tq, LANES), lambda bb, h, qi: (bb * qpt + qi, OFF_QR // LANES + h // 2)),
                  tab, tab,
                  pl.BlockSpec((None, None, s, 2 * LANES), lambda bb, h, qi: (bb, h, 0, 0)),
                  pl.BlockSpec((None, None, s // kvb, VDIM, kvb), lambda bb, h, qi: (bb, h, 0, 0, 0))],
        out_specs=pl.BlockSpec((tq, VDIM), lambda bb, h, qi: (bb * qpt + qi, h)),
        scratch_shapes=[pltpu.VMEM((2 * LANES, tq), BF16), pltpu.VMEM((nq, kvb, Q_BLOCK), F32),
                        pltpu.VMEM((nq, SUBLANES, Q_BLOCK), F32), pltpu.VMEM((nq, SUBLANES, Q_BLOCK), F32),
                        pltpu.VMEM((nq, VDIM, Q_BLOCK), F32)],
        compiler_params=_params("parallel", "parallel", "arbitrary"),
        name="mla_attention",
    )(u_main, u_main, cs_tab, sn_tab, k, vt)


HALO = BF16_ROWS
CONV_COLS = 512
assert all(o % CONV_COLS == 0 for o in (OFF_XBC, OFF_SCB, OFF_SCC, OFF_SCX))


def _fill_ext(ext_ref, prev, cur, nxt, first, last):
    tm = cur.shape[0]
    ext_ref[:HALO, :] = jnp.where(first, 0.0, prev.astype(F32))
    ext_ref[HALO:HALO + tm, :] = cur.astype(F32)
    ext_ref[HALO + tm:, :] = jnp.where(last, 0.0, nxt.astype(F32))


def _taps(ext_ref, w_ref, ksize, tm):
    pad = ksize // 2
    acc = None
    for kk in range(ksize):
        term = ext_ref[pl.ds(HALO - pad + kk, tm), :] * w_ref[kk:kk + 1, :]
        acc = term if acc is None else acc + term
    return acc


def _ssm_conv_kernel(prev_ref, cur_ref, next_ref, w_ref, b_ref, o_ref, ext_ref, *, tiles_per_seq):
    i = pl.program_id(0)
    tm = cur_ref.shape[0]
    _fill_ext(ext_ref, prev_ref[...], cur_ref[...], next_ref[...],
              i % tiles_per_seq == 0, i % tiles_per_seq == tiles_per_seq - 1)
    o_ref[...] = _silu(_taps(ext_ref, w_ref, SSM_CONV, tm) + b_ref[...]).astype(o_ref.dtype)


def _halo_specs(tm, tc, col0, n_rows):
    r = tm // HALO
    last = n_rows // HALO - 1
    cb = col0 // tc
    return [pl.BlockSpec((HALO, tc), lambda i, j: (jnp.maximum(i * r - 1, 0), cb + j)),
            pl.BlockSpec((tm, tc), lambda i, j: (i, cb + j)),
            pl.BlockSpec((HALO, tc), lambda i, j: (jnp.minimum((i + 1) * r, last), cb + j))]


def _ssm_conv(u_main, conv_w, conv_b, s):
    t = u_main.shape[0]
    tm = min(s, 512)
    tc = CONV_COLS
    kern = functools.partial(_ssm_conv_kernel, tiles_per_seq=s // tm)
    return pl.pallas_call(
        kern,
        out_shape=jax.ShapeDtypeStruct((t, XBC), BF16),
        grid=(t // tm, XBC // tc),
        in_specs=_halo_specs(tm, tc, OFF_XBC, t) + [
            pl.BlockSpec((SSM_CONV, tc), lambda i, j: (0, j)),
            pl.BlockSpec((1, tc), lambda i, j: (0, j))],
        out_specs=pl.BlockSpec((tm, tc), lambda i, j: (i, j)),
        scratch_shapes=[pltpu.VMEM((tm + 2 * HALO, tc), F32)],
        compiler_params=_params("parallel", "parallel"),
        name="ssm_conv",
    )(u_main, u_main, u_main, conv_w, conv_b.reshape(1, XBC))


def _sconv_kernel(cp_ref, cc_ref, cn_ref, xp_ref, xc_ref, xn_ref, b_ref, w_ref, o_ref, ext_ref, *, tiles_per_seq):
    i = pl.program_id(0)
    tm = cc_ref.shape[0]
    f = lambda r: r[...].astype(F32)
    _fill_ext(ext_ref, f(cp_ref) * f(xp_ref), f(cc_ref) * f(xc_ref), f(cn_ref) * f(xn_ref),
              i % tiles_per_seq == 0, i % tiles_per_seq == tiles_per_seq - 1)
    o_ref[...] = (f(b_ref) * _taps(ext_ref, w_ref, SCONV_K, tm)).astype(o_ref.dtype)


def _short_conv(u_main, sconv_w, s):
    t = u_main.shape[0]
    tm = min(s, 512)
    tc = CONV_COLS
    kern = functools.partial(_sconv_kernel, tiles_per_seq=s // tm)
    return pl.pallas_call(
        kern,
        out_shape=jax.ShapeDtypeStruct((t, D), BF16),
        grid=(t // tm, D // tc),
        in_specs=_halo_specs(tm, tc, OFF_SCC, t) + _halo_specs(tm, tc, OFF_SCX, t) + [
            pl.BlockSpec((tm, tc), lambda i, j: (i, OFF_SCB // tc + j)),
            pl.BlockSpec((SCONV_K, tc), lambda i, j: (0, j))],
        out_specs=pl.BlockSpec((tm, tc), lambda i, j: (i, j)),
        scratch_shapes=[pltpu.VMEM((tm + 2 * HALO, tc), F32)],
        compiler_params=_params("parallel", "parallel"),
        name="short_conv",
    )(u_main, u_main, u_main, u_main, u_main, u_main, u_main, sconv_w)


FWD_COL = 2 * SSM_HEADS
BWD_COL = 3 * SSM_HEADS
PAIRS = SSM_HEADS // 2
PAIRS_PER_GROUP = PAIRS // SSM_GROUPS


def _cumsum(x, axis, reverse):
    n = x.shape[axis]
    idx = lax.broadcasted_iota(I32, x.shape, axis)
    k = 1
    while k < n:
        if reverse:
            x = x + jnp.where(idx < n - k, pltpu.roll(x, n - k, axis), 0.0)
        else:
            x = x + jnp.where(idx >= k, pltpu.roll(x, k, axis), 0.0)
        k *= 2
    return x


def _spread_heads(vals, sel, base, exact):
    lane = _lane_iota(vals.shape)
    v = jnp.where(lane >= base, jnp.where(lane < base + SSM_HEADS, vals, 0.0), 0.0)
    hi = v.astype(BF16)
    out = _dot(hi, sel)
    if exact:
        rest = v - hi.astype(F32)
        mid = rest.astype(BF16)
        out = out + _dot(mid, sel) + _dot((rest - mid.astype(F32)).astype(BF16), sel)
    return out


def _ssd_direction(xbc_ref, us_ref, ust_ref, sel, dtb_row, a_row, dtb_col, a_col, d_row, y_ref, st_ref, *, reverse):
    base = BWD_COL if reverse else FWD_COL
    edge = 0 if reverse else CHUNK - 1
    dt_c = _softplus(us_ref[...] + dtb_row)
    cum_c = _cumsum(dt_c * a_row, 0, reverse)
    dt_r = _softplus(ust_ref[...] + dtb_col)
    cum_r = _cumsum(dt_r * a_col, 1, reverse)
    tot = cum_c[edge:edge + 1, :]
    stacked = jnp.concatenate([dt_c, jnp.exp(cum_c), jnp.exp(tot - cum_c) * dt_c], axis=0)
    spread = _spread_heads(stacked, sel, base, False)
    dt_x, ecum_x, dec_x = spread[:CHUNK], spread[CHUNK:2 * CHUNK], spread[2 * CHUNK:]
    etot_x = _spread_heads(jnp.broadcast_to(jnp.exp(tot), (SUBLANES, LANES)), sel, base, True)[0:1, :]

    li = lax.broadcasted_iota(I32, (CHUNK, CHUNK), 0)
    si = lax.broadcasted_iota(I32, (CHUNK, CHUNK), 1)
    mask = (li <= si) if reverse else (li >= si)
    lane = _lane_iota((CHUNK, LANES))
    lo = lane < SSM_P

    for g in range(SSM_GROUPS):
        bg = xbc_ref[:, SSM_INNER + g * SSM_N:SSM_INNER + (g + 1) * SSM_N]
        cg = xbc_ref[:, SSM_INNER + (SSM_GROUPS + g) * SSM_N:SSM_INNER + (SSM_GROUPS + g + 1) * SSM_N]
        cb = _dot_nt(cg, bg)
        for j in range(PAIRS_PER_GROUP):
            pr = g * PAIRS_PER_GROUP + j
            c1, c2 = base + 2 * pr, base + 2 * pr + 1
            cols = slice(pr * LANES, (pr + 1) * LANES)

            xs = xbc_ref[:, cols].astype(F32)
            xdt = xs * dt_x[:, cols]
            scores = []
            for c in (c1, c2):
                seg = cum_c[:, c:c + 1] - cum_r[c:c + 1, :]
                scores.append((cb * jnp.exp(jnp.where(mask, seg, -jnp.inf))).astype(BF16))
            xb = xdt.astype(BF16)
            zero = jnp.zeros_like(xb)
            x_bd = jnp.concatenate([jnp.where(lo, xb, zero), jnp.where(lo, zero, xb)], axis=0)
            y = _dot(jnp.concatenate(scores, axis=1), x_bd)
            st = st_ref[g, :, j * LANES:(j + 1) * LANES]
            y = y + _dot(cg, st.astype(BF16)) * ecum_x[:, cols]
            if not reverse:
                y = y + xs * d_row[:, cols]
            y_ref[:, cols] = y.astype(y_ref.dtype)
            new = _dot_tn(bg, (xs * dec_x[:, cols]).astype(BF16))
            st_ref[g, :, j * LANES:(j + 1) * LANES] = st * etot_x[:, cols] + new


def _ssd_kernel(xf_ref, usf_ref, ustf_ref, xb_ref, usb_ref, ustb_ref, sel_ref, dtb_row_ref, alog_row_ref,
                dtb_col_ref, alog_col_ref, d_ref, yf_ref, yb_ref, stf_ref, stb_ref):
    @pl.when(pl.program_id(1) == 0)
    def _():
        stf_ref[...] = jnp.zeros(stf_ref.shape, F32)
        stb_ref[...] = jnp.zeros(stb_ref.shape, F32)

    dtb_row, dtb_col = dtb_row_ref[...], dtb_col_ref[...]
    a_row, a_col = -jnp.exp(alog_row_ref[...]), -jnp.exp(alog_col_ref[...])
    d_row = d_ref[...]
    sel = sel_ref[...]
    _ssd_direction(xf_ref, usf_ref, ustf_ref, sel, dtb_row, a_row, dtb_col, a_col, d_row, yf_ref, stf_ref,
                   reverse=False)
    _ssd_direction(xb_ref, usb_ref, ustb_ref, sel, dtb_row, a_row, dtb_col, a_col, d_row, yb_ref, stb_ref,
                   reverse=True)


def _ssd_scan(xbc_conv, u_small, u_small_t, a_log, dt_bias, d_skip, b, s):
    t = b * s
    nc = s // CHUNK
    pad = jnp.zeros((2 * SSM_HEADS,), F32)
    dtb = jnp.concatenate([pad, dt_bias.reshape(-1)])
    alog = jnp.concatenate([pad, a_log.reshape(-1)])
    d_row = jnp.repeat(d_skip, SSM_P)[None, :]
    head_of_lane = np.concatenate([np.full(2 * SSM_HEADS, -1), np.arange(SSM_HEADS), np.arange(SSM_HEADS)])
    sel = jnp.asarray(head_of_lane[:, None] == np.repeat(np.arange(SSM_HEADS), SSM_P)[None, :], BF16)
    fw = lambda bb, i: bb * nc + i
    bw = lambda bb, i: bb * nc + nc - 1 - i
    row = lambda w: pl.BlockSpec((1, w), lambda bb, i: (0, 0))
    col = pl.BlockSpec((LANES, 1), lambda bb, i: (0, 0))

    def chunk_specs(idx):
        return [pl.BlockSpec((CHUNK, XBC), lambda bb, i: (idx(bb, i), 0)),
                pl.BlockSpec((CHUNK, LANES), lambda bb, i: (idx(bb, i), 0)),
                pl.BlockSpec((LANES, CHUNK), lambda bb, i: (0, idx(bb, i)))]

    return pl.pallas_call(
        _ssd_kernel,
        out_shape=(jax.ShapeDtypeStruct((t, SSM_INNER), BF16),) * 2,
        grid=(b, nc),
        in_specs=chunk_specs(fw) + chunk_specs(bw) + [
            pl.BlockSpec((LANES, SSM_INNER), lambda bb, i: (0, 0)),
            row(LANES), row(LANES), col, col, row(SSM_INNER)],
        out_specs=(pl.BlockSpec((CHUNK, SSM_INNER), lambda bb, i: (fw(bb, i), 0)),
                   pl.BlockSpec((CHUNK, SSM_INNER), lambda bb, i: (bw(bb, i), 0))),
        scratch_shapes=[pltpu.VMEM((SSM_GROUPS, SSM_N, SSM_INNER // SSM_GROUPS), F32)] * 2,
        compiler_params=_params("parallel", "arbitrary"),
        name="ssd_scan",
    )(xbc_conv, u_small, u_small_t, xbc_conv, u_small, u_small_t, sel,
      dtb[None, :], alog[None, :], dtb[:, None], alog[:, None], d_row)


def _ssm_post_kernel(yf_ref, yb_ref, z_ref, w_ref, o_ref):
    y = (yf_ref[...].astype(F32) + yb_ref[...].astype(F32)) * _silu(z_ref[...].astype(F32))
    o_ref[...] = _rms(y, w_ref[...]).astype(o_ref.dtype)


def _ssm_post(y_fwd, y_bwd, u_main, norm_w):
    t = y_fwd.shape[0]
    tm = min(t, 512)
    gw = SSM_INNER // SSM_GROUPS
    blk = pl.BlockSpec((tm, gw), lambda i, g: (i, g))
    return pl.pallas_call(
        _ssm_post_kernel,
        out_shape=jax.ShapeDtypeStruct((t, SSM_INNER), BF16),
        grid=(t // tm, SSM_GROUPS),
        in_specs=[blk, blk, pl.BlockSpec((tm, gw), lambda i, g: (i, OFF_Z // gw + g)),
                  pl.BlockSpec((1, gw), lambda i, g: (0, g))],
        out_specs=blk,
        compiler_params=_params("parallel", "parallel"),
        name="ssm_gate_norm",
    )(y_fwd, y_bwd, u_main, norm_w.reshape(1, SSM_INNER))


def _merge_kernel(ya_ref, ys_ref, yc_ref, pa_ref, ps_ref, pc_ref, ga_ref, gs_ref, gc_ref, o_ref):
    g = lambda r: _sigmoid(r[...].astype(F32))
    m = (g(ga_ref) * _dot(ya_ref[...], pa_ref[...]) + g(gs_ref) * _dot(ys_ref[...], ps_ref[...])
         + g(gc_ref) * _dot(yc_ref[...], pc_ref[...]))
    o_ref[...] = m.astype(o_ref.dtype)


def _merge(y_att, y_ssm, y_conv, w_branch, u_main):
    t = y_att.shape[0]
    tm = min(t, 512)
    tn = 512
    yb = pl.BlockSpec((tm, D), lambda i, j: (i, 0))
    pspec = lambda r: pl.BlockSpec((D, tn), lambda i, j: (r, j))
    gspec = lambda r: pl.BlockSpec((tm, tn), lambda i, j: (i, OFF_GATE // tn + r * (D // tn) + j))
    return pl.pallas_call(
        _merge_kernel,
        out_shape=jax.ShapeDtypeStruct((t, D), BF16),
        grid=(t // tm, D // tn),
        in_specs=[yb, yb, yb, pspec(0), pspec(1), pspec(2), gspec(0), gspec(1), gspec(2)],
        out_specs=pl.BlockSpec((tm, tn), lambda i, j: (i, j)),
        compiler_params=_params("parallel", "arbitrary"),
        name="branch_merge",
    )(y_att, y_ssm, y_conv, w_branch, w_branch, w_branch, u_main, u_main, u_main)


def _split_bf16(x):
    hi = x.astype(BF16)
    lo = (x - hi.astype(F32)).astype(BF16)
    return hi, lo


def _wo_kernel(m_ref, h_ref, wo_ref, nw_ref, rhi_ref, rlo_ref, rb_ref, h1_ref, t_ref, lg_ref):
    h1 = h_ref[...] + _dot(m_ref[...], wo_ref[...])
    h1_ref[...] = h1
    tn = _rms(h1, nw_ref[...])
    for sl in range(SLAB):
        t_ref[pl.ds(sl, tn.shape[0], stride=SLAB), :] = tn[:, sl * LANES:(sl + 1) * LANES]
    hi, lo = _split_bf16(tn)
    lg_ref[...] = _dot(hi, rhi_ref[...]) + _dot(lo, rhi_ref[...]) + _dot(hi, rlo_ref[...]) + rb_ref[...]


def _out_proj(merged, h, w_o, ffn_norm_w, r_hi, r_lo, r_bias):
    t = h.shape[0]
    tm = min(t, 256)
    blk = pl.BlockSpec((tm, D), lambda i: (i, 0))
    const = lambda shape: pl.BlockSpec(shape, lambda i: (0, 0))
    lg = pl.BlockSpec((tm, LANES), lambda i: (i, 0))
    return pl.pallas_call(
        _wo_kernel,
        out_shape=(jax.ShapeDtypeStruct((t, D), F32), jax.ShapeDtypeStruct((t * SLAB, LANES), F32),
                   jax.ShapeDtypeStruct((t, LANES), F32)),
        grid=(t // tm,),
        in_specs=[blk, blk, const((D, D)), const((1, D)), const((D, LANES)), const((D, LANES)),
                  const((1, LANES))],
        out_specs=(blk, pl.BlockSpec((tm * SLAB, LANES), lambda i: (i, 0)), lg),
        compiler_params=_params("parallel"),
        name="out_proj_router",
    )(merged, h, w_o, ffn_norm_w.reshape(1, D), r_hi, r_lo, r_bias)


def _route_kernel(lg_ref, id_ref, wt_ref, rk_ref, cnt_ref, carry_sc):
    @pl.when(pl.program_id(0) == 0)
    def _():
        carry_sc[...] = jnp.zeros(carry_sc.shape, F32)

    lg = lg_ref[...]
    lane = _lane_iota(lg.shape).astype(F32)
    big = float(LANES)
    ninf = -jnp.inf
    first = lambda hit: jnp.min(jnp.where(hit, lane, big), axis=1, keepdims=True)

    gl = jnp.where(lane < N_GROUPS, lg, ninf)
    gmax = jnp.max(gl, axis=1, keepdims=True)
    gidx = first(gl == gmax)
    g_w = 1.0 / jnp.sum(jnp.exp(gl - gmax), axis=1, keepdims=True)

    in_group = (lane >= N_GROUPS + gidx * EPG) & (lane < N_GROUPS + (gidx + 1.0) * EPG)
    el = jnp.where(in_group, lg, ninf)
    v0 = jnp.max(el, axis=1, keepdims=True)
    i0 = first(el == v0)
    el = jnp.where(lane == i0, ninf, el)
    v1 = jnp.max(el, axis=1, keepdims=True)
    i1 = first(el == v1)
    e1 = jnp.exp(v1 - v0)
    w0 = g_w / (1.0 + e1)
    w1 = g_w * e1 / (1.0 + e1)
    id_ref[...] = jnp.where(lane == 0.0, i0, i1).astype(I32) - N_GROUPS
    wt_ref[...] = jnp.where(lane == 0.0, w0, w1)

    hit0 = lane == i0 - N_GROUPS
    hit1 = lane == i1 - N_GROUPS
    onehot = jnp.where(hit0, 1.0, jnp.where(hit1, 1.0, 0.0))
    tm = lg.shape[0]
    earlier = lax.broadcasted_iota(I32, (tm, tm), 1) < lax.broadcasted_iota(I32, (tm, tm), 0)
    carry = carry_sc[0:1, :]
    before = _dot(jnp.where(earlier, 1.0, 0.0).astype(BF16), onehot.astype(BF16)) + carry
    r0 = jnp.sum(jnp.where(hit0, before, 0.0), axis=1, keepdims=True)
    r1 = jnp.sum(jnp.where(hit1, before, 0.0), axis=1, keepdims=True)
    rk_ref[...] = jnp.where(lane == 0.0, r0, r1).astype(I32)
    carry_sc[...] = jnp.broadcast_to(carry + jnp.sum(onehot, axis=0, keepdims=True), carry_sc.shape)
    cnt_ref[...] = carry_sc[...]


def _route(logits):
    t = logits.shape[0]
    tm = min(t, 512)
    blk = pl.BlockSpec((tm, LANES), lambda i: (i, 0))
    cnt = pl.BlockSpec((SUBLANES, LANES), lambda i: (0, 0))
    return pl.pallas_call(
        _route_kernel,
        out_shape=(jax.ShapeDtypeStruct((t, LANES), I32), jax.ShapeDtypeStruct((t, LANES), F32),
                   jax.ShapeDtypeStruct((t, LANES), I32), jax.ShapeDtypeStruct((SUBLANES, LANES), F32)),
        grid=(t // tm,),
        in_specs=[blk],
        out_specs=(blk, blk, blk, cnt),
        scratch_shapes=[pltpu.VMEM((SUBLANES, LANES), F32)],
        compiler_params=_params("arbitrary"),
        name="route",
    )(logits)


def _dispatch_plan(ids, wts, ranks, counts, t, tile):
    n_rows = 2 * t
    n_tiles = n_rows // tile + N_EXPERTS
    n_pos = n_tiles * tile
    e = ids[:, :2].reshape(-1)
    w = wts[:, :2].reshape(-1)
    rank = ranks[:, :2].reshape(-1)
    counts = counts[0, :N_EXPERTS].astype(I32)
    padded = ((counts + tile - 1) // tile) * tile
    ends = jnp.cumsum(padded)
    starts = ends - padded
    pos = starts[e] + rank
    r = jnp.arange(n_rows, dtype=I32)
    packed = jnp.stack([r, lax.bitcast_convert_type(w, I32)], axis=1)
    slots = jnp.full((n_pos, 2), -1, I32).at[pos].set(packed)
    src = slots[:, 0]
    valid = src >= 0
    tile_start = jnp.arange(n_tiles, dtype=I32) * tile
    tile_e = jnp.minimum(jnp.sum(tile_start[:, None] >= ends[None, :], axis=1), N_EXPERTS - 1).astype(I32)
    in_tile = jnp.arange(n_pos, dtype=I32) % tile
    row_tok = jnp.where(valid, src // 2, 0)
    row_dst = jnp.where(valid, (src % 2) * t + src // 2, n_rows + in_tile)
    row_w = jnp.where(valid, lax.bitcast_convert_type(slots[:, 1], F32), 0.0)
    n_used = (ends[-1] // tile).astype(I32).reshape(1)
    return (row_tok.reshape(n_tiles, 1, tile), row_dst.reshape(n_tiles, 1, tile),
            row_w.reshape(n_pos, 1), tile_e, n_used)


DMA_UNROLL = 8


def _moe_kernel(te_ref, nu_ref, tok_ref, tokn_ref, dstp_ref, dst_ref, rw_ref, t_hbm, wg_ref, wu_ref, wd_ref, o_hbm,
                xbuf, ybuf, wg_sc, wu_sc, wd_sc, gsem, ssem):
    i = pl.program_id(0)
    n_used = nu_ref[0]
    tile = rw_ref.shape[0]
    rows = tile * PITCH
    slot = i % 2

    def gather_copy(tok, r, slot_):
        dst = xbuf.at[pl.ds(pl.multiple_of(slot_ * rows + r * PITCH, SUBLANES), SLAB), :]
        return pltpu.make_async_copy(t_hbm.at[tok[0, 0, r]], dst, gsem.at[slot_])

    def scatter_copy(dst, r):
        src = ybuf.at[pl.ds(pl.multiple_of(r * PITCH, SUBLANES), SLAB), :]
        return pltpu.make_async_copy(src, o_hbm.at[dst[0, 0, r]], ssem)

    def wait_rows(buf, sem):
        done = buf.at[pl.ds(0, tile * SLAB), :]
        pltpu.make_async_copy(done, done, sem).wait()

    def looped(make):
        def body(r, c):
            make(r).start()
            return c
        lax.fori_loop(0, tile, body, 0, unroll=DMA_UNROLL)

    @pl.when(i == 0)
    def _():
        ybuf[...] = jnp.zeros(ybuf.shape, F32)
        looped(lambda r: gather_copy(tok_ref, r, 0))

    @pl.when(i < n_used)
    def _():
        @pl.when((i == 0) | (te_ref[i] != te_ref[jnp.maximum(i - 1, 0)]))
        def _():
            wg_sc[...] = wg_ref[...].astype(BF16)
            wu_sc[...] = wu_ref[...].astype(BF16)
            wd_sc[...] = wd_ref[...].astype(BF16)

        @pl.when(i + 1 < n_used)
        def _():
            looped(lambda r: gather_copy(tokn_ref, r, 1 - slot))

        wait_rows(xbuf, gsem.at[slot])
        base = slot * rows
        x = jnp.concatenate([xbuf[pl.ds(base + sl, tile, stride=PITCH), :] for sl in range(SLAB)],
                            axis=1).astype(BF16)
        for r in range(tile):
            scatter_copy(dstp_ref, r).start()
        gate = _dot(x, wg_sc[...])
        mid = (_silu(gate) * _dot(x, wu_sc[...])).astype(BF16)
        y = _dot(mid, wd_sc[...]) * rw_ref[...]
        wait_rows(ybuf, ssem)
        for sl in range(SLAB):
            ybuf[pl.ds(sl, tile, stride=PITCH), :] = y[:, sl * LANES:(sl + 1) * LANES]

        @pl.when(i == n_used - 1)
        def _():
            looped(lambda r: scatter_copy(dst_ref, r))
            wait_rows(ybuf, ssem)


def _moe(t_norm, plan, layer, w_gate, w_up, w_down):
    t = t_norm.shape[0] // SLAB
    row_tok, row_dst, row_w, tile_e, n_used = plan
    n_tiles, _, tile = row_tok.shape
    t_slab = t_norm.reshape(t, SLAB, LANES)
    extra = (2 * t + jnp.arange(tile, dtype=I32)).reshape(1, 1, tile)
    dst_prev = jnp.concatenate([extra, row_dst[:-1]], axis=0)
    smem = lambda f: pl.BlockSpec((1, 1, tile), lambda i, te, nu: (f(i), 0, 0), memory_space=pltpu.SMEM)
    wspec = lambda a, b: pl.BlockSpec((None, None, a, b), lambda i, te, nu: (layer, te[i], 0, 0))
    out = pl.pallas_call(
        _moe_kernel,
        out_shape=jax.ShapeDtypeStruct((2 * t + tile, SLAB, LANES), F32),
        grid_spec=pltpu.PrefetchScalarGridSpec(
            num_scalar_prefetch=2,
            grid=(n_tiles,),
            in_specs=[smem(lambda i: i), smem(lambda i: jnp.minimum(i + 1, n_tiles - 1)),
                      smem(lambda i: i), smem(lambda i: i),
                      pl.BlockSpec((tile, 1), lambda i, te, nu: (i, 0)),
                      pl.BlockSpec(memory_space=pl.ANY),
                      wspec(D, D_EXPERT), wspec(D, D_EXPERT), wspec(D_EXPERT, D)],
            out_specs=pl.BlockSpec(memory_space=pl.ANY),
            scratch_shapes=[pltpu.VMEM((2 * tile * PITCH, LANES), F32), pltpu.VMEM((tile * PITCH, LANES), F32),
                            pltpu.VMEM((D, D_EXPERT), BF16), pltpu.VMEM((D, D_EXPERT), BF16),
                            pltpu.VMEM((D_EXPERT, D), BF16),
                            pltpu.SemaphoreType.DMA((2,)), pltpu.SemaphoreType.DMA(())]),
        compiler_params=_params("arbitrary"),
        name="moe_experts",
    )(tile_e, n_used, row_tok, row_tok, dst_prev, row_dst, row_w, t_slab, w_gate, w_up, w_down)
    return out.reshape((2 * t + tile) * SLAB, LANES)


def kernel(x, positions, attn_norm_w, w_in, kv_norm_w, w_uk, w_uv, ssm_conv_w, ssm_conv_b, ssm_A_log, ssm_dt_bias, ssm_D, ssm_norm_w, sconv_w, w_branch, w_o, ffn_norm_w, router_group_w, router_group_b, router_expert_w, router_expert_b, expert_w_gate, expert_w_up, expert_w_down, final_norm_w):
    b, s, _ = x.shape
    t = b * s
    depth = w_in.shape[0]
    h = x.reshape(t, D)
    cs_tab, sn_tab = _rope_tables(positions)
    moe_out = None
    for l in range(depth):
        if moe_out is None:
            n = _norm(h, attn_norm_w[l], BF16)
        else:
            h, n = _add_norm(h, moe_out, attn_norm_w[l], BF16)
        w_perm = _permute_in_proj(w_in[l])
        w_small = w_perm[:, OFF_SMALL:OFF_SMALL + LANES]
        u_main, u_small, u_small_t = _in_proj(n, w_perm, w_small, w_small.T)

        k, vt = _kv_proj(u_main, u_small, kv_norm_w[l], w_uk[l].astype(BF16), w_uv[l].T.astype(BF16),
                         cs_tab, sn_tab, b, s)
        y_att = _attention(u_main, k, vt, cs_tab, sn_tab, b, s)

        xbc_conv = _ssm_conv(u_main, ssm_conv_w[l], ssm_conv_b[l], s)
        y_fwd, y_bwd = _ssd_scan(xbc_conv, u_small, u_small_t, ssm_A_log[l], ssm_dt_bias[l], ssm_D[l], b, s)
        y_ssm = _ssm_post(y_fwd, y_bwd, u_main, ssm_norm_w[l])

        y_conv = _short_conv(u_main, sconv_w[l], s)

        merged = _merge(y_att, y_ssm, y_conv, w_branch[l].astype(BF16), u_main)

        r_w = jnp.concatenate([router_group_w[l], router_expert_w[l],
                               jnp.zeros((D, LANES - N_GROUPS - N_EXPERTS), F32)], axis=1)
        r_b = jnp.concatenate([router_group_b[l], router_expert_b[l],
                               jnp.zeros((LANES - N_GROUPS - N_EXPERTS,), F32)])[None, :]
        r_hi, r_lo = _split_bf16(r_w)
        h, t_norm, logits = _out_proj(merged, h, w_o[l].astype(BF16), ffn_norm_w[l], r_hi, r_lo, r_b)

        ids, wts, ranks, counts = _route(logits)
        plan = _dispatch_plan(ids, wts, ranks, counts, t, MOE_TILE)
        moe_out = _moe(t_norm, plan, l, expert_w_gate, expert_w_up, expert_w_down)

    h, out = _add_norm(h, moe_out, final_norm_w, F32)
    return out.reshape(b, s, D)
```
